```python
import jax, jax.numpy as jnp
from jax import lax
import numpy as np

D_MODEL = 1024
BATCH = 4
SEQ = 8192
DEPTH = 1
DEC_BATCH = 16
DEC_SEQ = 64
PAST_LEN = 2048

CHUNK = 64
N_META = 16
N_HEADS_A = 8
HEAD_DIM = 128
ROPE_DIM_A = HEAD_DIM // 4
N_HEADS_I = 8
HEAD_DIM_I = 64
ROPE_DIM_I = HEAD_DIM_I // 4
TOPK_MAX = 256
ROPE_THETA = 500000.0
Q_BLOCK = 128
RW_HEAD = 64
RW_WIDTH = D_MODEL
RW_HEADS = RW_WIDTH // RW_HEAD
LORA_W = 64
LORA_A = 64
LORA_G = 128
RW_GN_EPS = 64e-5
D_FF = -(-8 * D_MODEL // (3 * 256)) * 256
LN_EPS = 1e-5
ALPHA = (2 * DEPTH) ** 0.25
BETA = (8 * DEPTH) ** -0.25
NEG = -1e30
ATTN_COLS = N_HEADS_A * HEAD_DIM
RW_SIZES = (RW_WIDTH, RW_WIDTH, RW_WIDTH, LORA_W, LORA_A, LORA_G)
SHIFT_COLS = sum(RW_SIZES)
IN_SIZES = (ATTN_COLS, HEAD_DIM, HEAD_DIM, N_HEADS_I * HEAD_DIM_I, HEAD_DIM_I, N_HEADS_I, 2 * D_MODEL, SHIFT_COLS)
IN_COLS = sum(IN_SIZES)

kernel_name = 'dsa_rwkv7_gated_hybrid_stream_step'


def split_cols(p, sizes):
    return jnp.split(p, np.cumsum(sizes)[:-1].tolist(), axis=-1)


def layernorm(x, g, b):
    xf = x.astype(jnp.float32)
    mu = xf.mean(-1, keepdims=True)
    var = jnp.square(xf - mu).mean(-1, keepdims=True)
    return ((xf - mu) * lax.rsqrt(var + LN_EPS) * g + b).astype(x.dtype)


def rope_tables(pos, rot_dim):
    inv = ROPE_THETA ** (-jnp.arange(0, rot_dim, 2, dtype=jnp.float32) / rot_dim)
    ang = pos.astype(jnp.float32)[:, None] * inv[None]
    return jnp.cos(ang)[:, None, :], jnp.sin(ang)[:, None, :]


def apply_rope(x, cos, sin):
    half = cos.shape[-1]
    x1 = x[..., :half].astype(jnp.float32)
    x2 = x[..., half:2 * half].astype(jnp.float32)
    rot = jnp.concatenate([x1 * cos - x2 * sin, x2 * cos + x1 * sin], axis=-1).astype(x.dtype)
    return jnp.concatenate([rot, x[..., 2 * half:]], axis=-1)


def gather_rows(rows, idx):
    return jax.vmap(lambda r, i: r[i])(rows, idx)


def dsa_block(q, qi, wi, qc, K, V, KI, kc, n_sel):
    idx_logits = jnp.einsum('bqhd,bsd->bqhs', qi, KI) * HEAD_DIM_I ** -0.5
    score = jnp.einsum('bqhs,bqh->bqs', jax.nn.relu(idx_logits), wi).astype(jnp.float32)
    admissible = kc[None, :] <= qc[:, None]
    score = jnp.where(admissible[None], score, NEG)
    top_val, top_idx = lax.top_k(score, n_sel)
    valid = top_val > 0.5 * NEG
    k_sel = gather_rows(K, top_idx)
    v_sel = gather_rows(V, top_idx)
    logits = jnp.einsum('bqhd,bqkd->bqhk', q, k_sel).astype(jnp.float32) * HEAD_DIM ** -0.5
    logits = jnp.where(valid[:, :, None, :], logits, NEG)
    prob = jax.nn.softmax(logits, axis=-1).astype(v_sel.dtype)
    return jnp.einsum('bqhk,bqkd->bqhd', prob, v_sel)


def dsa_attention(q, qi, wi, qc, K, V, KI, kc, n_sel):
    B, T = q.shape[:2]
    if T <= Q_BLOCK:
        return dsa_block(q, qi, wi, qc, K, V, KI, kc, n_sel)
    n_blk = -(-T // Q_BLOCK)
    pad = n_blk * Q_BLOCK - T

    def to_blocks(a):
        a = jnp.pad(a, [(0, 0), (0, pad)] + [(0, 0)] * (a.ndim - 2))
        return jnp.moveaxis(a.reshape((B, n_blk, Q_BLOCK) + a.shape[2:]), 1, 0)

    qc_b = jnp.pad(qc, (0, pad)).reshape(n_blk, Q_BLOCK)
    out = lax.map(lambda args: dsa_block(args[0], args[1], args[2], args[3], K, V, KI, kc, n_sel),
                  (to_blocks(q), to_blocks(qi), to_blocks(wi), qc_b))
    out = jnp.moveaxis(out, 0, 1).reshape(B, n_blk * Q_BLOCK, N_HEADS_A, HEAD_DIM)
    return out[:, :T]


def wkv7_scan(r, decay, k, v, a, b, S0):
    tm = lambda t: jnp.moveaxis(t.astype(jnp.float32), 1, 0)

    def step(S, inp):
        r_t, w_t, k_t, v_t, a_t, b_t = inp
        sa = jnp.einsum('bhij,bhj->bhi', S, a_t)
        S = S * w_t[:, :, None, :] + sa[..., None] * b_t[:, :, None, :] + v_t[..., None] * k_t[:, :, None, :]
        return S, jnp.einsum('bhij,bhj->bhi', S, r_t)

    S_fin, y = lax.scan(step, S0.astype(jnp.float32), (tm(r), tm(decay), tm(k), tm(v), tm(a), tm(b)))
    return jnp.moveaxis(y, 0, 1), S_fin.astype(S0.dtype)


def head_groupnorm(y, g, b):
    mu = y.mean(-1, keepdims=True)
    var = jnp.square(y - mu).mean(-1, keepdims=True)
    return (y - mu) * lax.rsqrt(var + RW_GN_EPS) * g.reshape(RW_HEADS, RW_HEAD) + b.reshape(RW_HEADS, RW_HEAD)


def rwkv7_branch(rw, shift0, wkv0, lp):
    B, T, _ = rw.shape
    prev = jnp.concatenate([shift0[:, None, :], rw[:, :-1]], axis=1)
    xs = rw + (prev - rw) * lp['rw_mu']
    r, k, v, lw, la, lg = split_cols(xs, RW_SIZES)
    w_log = -jax.nn.softplus(-(lp['rw_w0'] + jnp.tanh(lw) @ lp['rw_w2'])) - 0.5
    decay = jnp.exp(-jnp.exp(w_log.astype(jnp.float32)))
    a = jax.nn.sigmoid(lp['rw_a0'] + la @ lp['rw_a2'])
    g = jax.nn.sigmoid(lg) @ lp['rw_g2']
    heads = lambda t: t.reshape(B, T, RW_HEADS, RW_HEAD)
    kk = heads((k * lp['rw_k_k']).astype(jnp.float32))
    kk = kk / jnp.maximum(jnp.sqrt(jnp.sum(kk * kk, axis=-1, keepdims=True)), 1e-12)
    k = k * (1 + (a - 1) * lp['rw_k_a'])
    r_h, k_h, v_h = heads(r), heads(k), heads(v)
    y, wkv_new = wkv7_scan(r_h, heads(decay), k_h, v_h, -kk, kk * heads(a), wkv0)
    y = head_groupnorm(y, lp['rw_gn_g'], lp['rw_gn_b'])
    bonus = jnp.sum((r_h * k_h * lp['rw_r_k']).astype(jnp.float32), axis=-1, keepdims=True) * v_h
    out = ((y + bonus).reshape(B, T, RW_WIDTH) * g).astype(rw.dtype)
    return out @ lp['w_o_rwkv'], wkv_new


def hybrid_layer(h, pos, q_chunk, past_k, past_v, past_ik, past_chunk, wkv0, shift0, n_sel, lp):
    B, T, _ = h.shape
    q, k, v, qi, ki, wi, gate_logits, rw = split_cols(h @ lp['w_in'], IN_SIZES)
    cos_a, sin_a = rope_tables(pos, ROPE_DIM_A)
    cos_i, sin_i = rope_tables(pos, ROPE_DIM_I)
    q = apply_rope(q.reshape(B, T, N_HEADS_A, HEAD_DIM), cos_a, sin_a)
    k = apply_rope(k[:, :, None, :], cos_a, sin_a)[:, :, 0]
    qi = apply_rope(qi.reshape(B, T, N_HEADS_I, HEAD_DIM_I), cos_i, sin_i)
    ki = layernorm(ki, lp['idx_k_ln_g'], lp['idx_k_ln_b'])
    ki = apply_rope(ki[:, :, None, :], cos_i, sin_i)[:, :, 0]
    keys_k = jnp.concatenate([past_k, k], axis=1)
    keys_v = jnp.concatenate([past_v, v], axis=1)
    keys_ik = jnp.concatenate([past_ik, ki], axis=1)
    key_chunk = jnp.concatenate([past_chunk, q_chunk])
    o_attn = dsa_attention(q, qi, wi * N_HEADS_I ** -0.5, q_chunk, keys_k, keys_v, keys_ik, key_chunk, n_sel)
    o_attn = o_attn.reshape(B, T, ATTN_COLS) @ lp['w_o_attn']
    o_rw, wkv_new = rwkv7_branch(rw, shift0, wkv0, lp)
    g_attn, g_rw = jnp.split(jax.nn.sigmoid(gate_logits), 2, axis=-1)
    mixed = (g_attn * o_attn + g_rw * o_rw) @ lp['w_out']
    x1 = layernorm(ALPHA * h + mixed, lp['ln1_g'], lp['ln1_b'])
    ffn = (jax.nn.silu(x1 @ lp['ffn_w_gate']) * (x1 @ lp['ffn_w_up'])) @ lp['ffn_w_down']
    x2 = layernorm(ALPHA * x1 + ffn, lp['ln2_g'], lp['ln2_b'])
    return x2, k, v, ki, wkv_new, rw[:, -1]


def setup_inputs(seed: int = 0) -> dict:
    key = jax.random.key(seed)
    keys = iter(jax.random.split(key, 48))

    def nrm(shape, scale=1.0):
        return scale * jax.random.normal(next(keys), shape, jnp.float32)

    def unif(shape, lo, hi):
        return jax.random.uniform(next(keys), shape, jnp.float32, lo, hi)

    L = DEPTH
    P = N_META + PAST_LEN
    return {
        'x_prompt': nrm((BATCH, SEQ, D_MODEL)),
        'x_sample': nrm((DEC_BATCH, DEC_SEQ, D_MODEL)),
        'cache_k': nrm((L, DEC_BATCH, P, HEAD_DIM)),
        'cache_v': nrm((L, DEC_BATCH, P, HEAD_DIM)),
        'cache_idx_k': nrm((L, DEC_BATCH, P, HEAD_DIM_I)),
        'state_wkv': nrm((L, DEC_BATCH, RW_HEADS, RW_HEAD, RW_HEAD), 0.5),
        'state_shift': nrm((L, DEC_BATCH, SHIFT_COLS)),
        'meta_tokens': nrm((N_META, D_MODEL)),
        'ln0_g': 1.0 + nrm((D_MODEL,), 0.02),
        'ln0_b': nrm((D_MODEL,), 0.02),
        'w_in': nrm((L, D_MODEL, IN_COLS), D_MODEL ** -0.5),
        'idx_k_ln_g': 1.0 + nrm((L, HEAD_DIM_I), 0.02),
        'idx_k_ln_b': nrm((L, HEAD_DIM_I), 0.02),
        'rw_mu': unif((L, SHIFT_COLS), 0.0, 1.0),
        'rw_w0': unif((L, RW_WIDTH), -5.0, -1.0),
        'rw_w2': nrm((L, LORA_W, RW_WIDTH), LORA_W ** -0.5),
        'rw_a0': nrm((L, RW_WIDTH), 0.1),
        'rw_a2': nrm((L, LORA_A, RW_WIDTH), LORA_A ** -0.5),
        'rw_g2': nrm((L, LORA_G, RW_WIDTH), LORA_G ** -0.5),
        'rw_k_k': 0.85 + nrm((L, RW_WIDTH), 0.05),
        'rw_k_a': 1.0 + nrm((L, RW_WIDTH), 0.05),
        'rw_r_k': nrm((L, RW_HEADS, RW_HEAD), 0.1),
        'rw_gn_g': 1.0 + nrm((L, RW_WIDTH), 0.02),
        'rw_gn_b': nrm((L, RW_WIDTH), 0.02),
        'w_o_attn': nrm((L, ATTN_COLS, D_MODEL), BETA * ATTN_COLS ** -0.5),
        'w_o_rwkv': nrm((L, RW_WIDTH, D_MODEL), BETA * RW_WIDTH ** -0.5),
        'w_out': nrm((L, D_MODEL, D_MODEL), BETA * D_MODEL ** -0.5),
        'ln1_g': 1.0 + nrm((L, D_MODEL), 0.02),
        'ln1_b': nrm((L, D_MODEL), 0.02),
        'ffn_w_gate': nrm((L, D_MODEL, D_FF), D_MODEL ** -0.5),
        'ffn_w_up': nrm((L, D_MODEL, D_FF), D_MODEL ** -0.5),
        'ffn_w_down': nrm((L, D_FF, D_MODEL), BETA * D_FF ** -0.5),
        'ln2_g': 1.0 + nrm((L, D_MODEL), 0.02),
        'ln2_b': nrm((L, D_MODEL), 0.02),
    }


def reference(x_prompt, x_sample, cache_k, cache_v, cache_idx_k, state_wkv, state_shift, meta_tokens,
              ln0_g, ln0_b, w_in, idx_k_ln_g, idx_k_ln_b, rw_mu, rw_w0, rw_w2, rw_a0, rw_a2, rw_g2,
              rw_k_k, rw_k_a, rw_r_k, rw_gn_g, rw_gn_b, w_o_attn, w_o_rwkv, w_out, ln1_g, ln1_b,
              ffn_w_gate, ffn_w_up, ffn_w_down, ln2_g, ln2_b):
    B, S, _ = x_prompt.shape
    Ts = x_sample.shape[1]
    P = cache_k.shape[2] - N_META
    dt = x_prompt.dtype
    meta = jnp.broadcast_to(meta_tokens.astype(dt)[None], (B, N_META, D_MODEL))
    h_p = layernorm(jnp.concatenate([meta, x_prompt], axis=1), ln0_g, ln0_b)
    h_s = layernorm(x_sample, ln0_g, ln0_b)
    meta_chunk = jnp.zeros((N_META,), jnp.int32)
    pos_p = jnp.arange(N_META + S, dtype=jnp.int32)
    chunk_p = jnp.concatenate([meta_chunk, 1 + jnp.arange(S, dtype=jnp.int32) // CHUNK])
    pos_s = N_META + P + jnp.arange(Ts, dtype=jnp.int32)
    chunk_s = 1 + (P + jnp.arange(Ts, dtype=jnp.int32)) // CHUNK
    chunk_past = jnp.concatenate([meta_chunk, 1 + jnp.arange(P, dtype=jnp.int32) // CHUNK])
    n_sel_p = min(TOPK_MAX, S // 4)
    n_sel_s = min(TOPK_MAX, (P + Ts) // 4)
    no_rows = jnp.zeros((B, 0, HEAD_DIM), dt)
    no_rows_i = jnp.zeros((B, 0, HEAD_DIM_I), dt)
    no_chunk = jnp.zeros((0,), jnp.int32)
    wkv_zero = jnp.zeros((B, RW_HEADS, RW_HEAD, RW_HEAD), dt)
    shift_zero = jnp.zeros((B, SHIFT_COLS), dt)

    kp_l, vp_l, ikp_l, sp_l, shp_l = [], [], [], [], []
    ks_l, vs_l, iks_l, ss_l, shs_l = [], [], [], [], []
    for l in range(DEPTH):
        lp = {
            'w_in': w_in[l], 'idx_k_ln_g': idx_k_ln_g[l], 'idx_k_ln_b': idx_k_ln_b[l],
            'rw_mu': rw_mu[l], 'rw_w0': rw_w0[l], 'rw_w2': rw_w2[l], 'rw_a0': rw_a0[l], 'rw_a2': rw_a2[l],
            'rw_g2': rw_g2[l], 'rw_k_k': rw_k_k[l], 'rw_k_a': rw_k_a[l], 'rw_r_k': rw_r_k[l],
            'rw_gn_g': rw_gn_g[l], 'rw_gn_b': rw_gn_b[l], 'w_o_attn': w_o_attn[l], 'w_o_rwkv': w_o_rwkv[l],
            'w_out': w_out[l], 'ln1_g': ln1_g[l], 'ln1_b': ln1_b[l], 'ffn_w_gate': ffn_w_gate[l],
            'ffn_w_up': ffn_w_up[l], 'ffn_w_down': ffn_w_down[l], 'ln2_g': ln2_g[l], 'ln2_b': ln2_b[l],
        }
        h_p, kp, vp, ikp, sp, shp = hybrid_layer(h_p, pos_p, chunk_p, no_rows, no_rows, no_rows_i, no_chunk,
                                                 wkv_zero, shift_zero, n_sel_p, lp)
        h_s, ks, vs, iks, ss, shs = hybrid_layer(h_s, pos_s, chunk_s, cache_k[l], cache_v[l], cache_idx_k[l],
                                                 chunk_past, state_wkv[l], state_shift[l], n_sel_s, lp)
        kp_l.append(kp); vp_l.append(vp); ikp_l.append(ikp); sp_l.append(sp); shp_l.append(shp)
        ks_l.append(ks); vs_l.append(vs); iks_l.append(iks); ss_l.append(ss); shs_l.append(shs)

    y_prompt = h_p[:, N_META:]
    y_sample = h_s
    k_prompt, v_prompt, idx_k_prompt = jnp.stack(kp_l), jnp.stack(vp_l), jnp.stack(ikp_l)
    wkv_prompt, shift_prompt = jnp.stack(sp_l), jnp.stack(shp_l)
    k_sample, v_sample, idx_k_sample = jnp.stack(ks_l), jnp.stack(vs_l), jnp.stack(iks_l)
    wkv_sample, shift_sample = jnp.stack(ss_l), jnp.stack(shs_l)
    return (y_prompt, y_sample, k_prompt, v_prompt, idx_k_prompt, wkv_prompt, shift_prompt,
            k_sample, v_sample, idx_k_sample, wkv_sample, shift_sample)
```

```python
import functools

import numpy as np
import jax
import jax.numpy as jnp
from jax import lax
from jax.experimental import pallas as pl
from jax.experimental.pallas import tpu as pltpu

F32 = jnp.float32
BF16 = jnp.bfloat16
I32 = jnp.int32

D_MODEL = 1024
N_META = 16
CHUNK = 64
N_HEADS_A = 8
HEAD_DIM = 128
ROPE_DIM_A = HEAD_DIM // 4
N_HEADS_I = 8
HEAD_DIM_I = 64
ROPE_DIM_I = HEAD_DIM_I // 4
TOPK_MAX = 256
ROPE_THETA = 500000.0
RW_HEAD = 64
RW_HEADS = 16
RW_PAIRS = RW_HEADS // 2
LORA_W = 64
LORA_A = 64
LORA_G = 128
RW_GN_EPS = 64e-5
D_FF = 2816
LN_EPS = 1e-5
ALPHA = 2.0 ** 0.25
NEG = -1e30
SHIFT_COLS = 3 * D_MODEL + LORA_W + LORA_A + LORA_G
_Q0, _K0, _V0, _QI0, _KI0, _WI0, _G0, _RW0 = 0, 1024, 1152, 1280, 1792, 1856, 1864, 3912
_IN_COLS = _RW0 + SHIFT_COLS

LANES = 128
KEY_TILE = 128
SCORE_ROWS = 512
ATT_ROWS = 256
INT_MIN = -2 ** 31
VMEM_LIMIT = 56 * 1024 * 1024

_NN = (((1,), (0,)), ((), ()))
_NT = (((1,), (1,)), ((), ()))
_TN = (((0,), (0,)), ((), ()))


def _mm(a, b, dn=_NN):
    return lax.dot_general(a, b, dn, preferred_element_type=F32)


def _split3(a):
    hi = a.astype(BF16)
    r1 = a - hi.astype(F32)
    mid = r1.astype(BF16)
    lo = (r1 - mid.astype(F32)).astype(BF16)
    return hi, mid, lo


def _dot_hi(a, b, dn=_NN):
    ah, am, _ = _split3(a)
    bh, bm, _ = _split3(b)
    return _mm(ah, bh, dn) + (_mm(ah, bm, dn) + _mm(am, bh, dn))


def _dot_exact_rhs(a, b_bf16, dn=_NN):
    ah, am, al = _split3(a)
    return _mm(ah, b_bf16, dn) + (_mm(am, b_bf16, dn) + _mm(al, b_bf16, dn))


def _layernorm(x, g, b):
    mu = jnp.mean(x, axis=-1, keepdims=True)
    d = x - mu
    var = jnp.mean(d * d, axis=-1, keepdims=True)
    return d * lax.rsqrt(var + LN_EPS) * g + b


def _const_spec(shape):
    nd = len(shape)
    return pl.BlockSpec(shape, lambda *_: (0,) * nd, pipeline_mode=pl.Buffered(1))


def _rope(x, cos, sin, period, half):
    width = x.shape[1]
    reps = width // LANES
    cc = jnp.concatenate([cos] * reps, axis=1) if reps > 1 else cos
    ss = jnp.concatenate([sin] * reps, axis=1) if reps > 1 else sin
    lane = lax.broadcasted_iota(I32, (1, width), 1) % period
    fwd = pltpu.roll(x, width - half, 1)
    bwd = pltpu.roll(x, half, 1)
    return x * cc + jnp.where(lane < half, fwd, bwd) * ss


def _proj_body(x_ref, tab_ref, g0_ref, b0_ref, wa_ref, wi_ref, wr_ref, ikg_ref, ikb_ref,
               q_ref, k_ref, v_ref, kbf_ref, vt_ref, qi_ref, kiwi_ref, kibf_ref, rw_ref):
    tm = x_ref.shape[0]
    hb = _layernorm(x_ref[...], g0_ref[...], b0_ref[...]).astype(BF16)
    tab = tab_ref[...]
    cos_a, sin_a = tab[:, 0:128], tab[:, 128:256]
    cos_i, sin_i = tab[:, 256:384], tab[:, 384:512]

    pa = _mm(hb, wa_ref[...])
    q = _rope(pa[:, :1024], cos_a, sin_a, HEAD_DIM, ROPE_DIM_A // 2)
    q_ref[...] = (q * HEAD_DIM ** -0.5).astype(BF16)
    k = _rope(pa[:, 1024:1152], cos_a, sin_a, HEAD_DIM, ROPE_DIM_A // 2)
    k_ref[...] = k
    kbf_ref[...] = k.astype(BF16)
    v = pa[:, 1152:1280]
    v_ref[...] = v
    for j in range(tm // KEY_TILE):
        vt_ref[j] = v[j * KEY_TILE:(j + 1) * KEY_TILE, :].T.astype(BF16)

    pi = _mm(hb, wi_ref[...])
    qi_ref[...] = _rope(pi[:, :512], cos_i, sin_i, HEAD_DIM_I, ROPE_DIM_I // 2).astype(BF16)
    kw = pi[:, 512:640]
    lane = lax.broadcasted_iota(I32, (1, LANES), 1)
    is_ki = lane < HEAD_DIM_I
    mu = jnp.sum(jnp.where(is_ki, kw, 0.0), axis=-1, keepdims=True) * (1.0 / HEAD_DIM_I)
    d = jnp.where(is_ki, kw - mu, 0.0)
    var = jnp.sum(d * d, axis=-1, keepdims=True) * (1.0 / HEAD_DIM_I)
    ki = d * lax.rsqrt(var + LN_EPS) * ikg_ref[...] + ikb_ref[...]
    ki = _rope(ki, cos_i, sin_i, HEAD_DIM_I, ROPE_DIM_I // 2)
    wi_scale = (N_HEADS_I ** -0.5) * (HEAD_DIM_I ** -0.5)
    kiwi_ref[...] = jnp.where(is_ki, ki, jnp.where(lane < HEAD_DIM_I + N_HEADS_I, kw * wi_scale, 0.0))
    kibf_ref[...] = ki[:, :HEAD_DIM_I].astype(BF16)

    rw_ref[...] = _mm(hb, wr_ref[...])


def _proj(x, tab, tab_blocks, tm, w):
    rows = x.shape[0]
    n = rows // tm
    row_spec = lambda width: pl.BlockSpec((tm, width), lambda i: (i, 0))
    out_shape = [
        jax.ShapeDtypeStruct((rows, 1024), BF16),
        jax.ShapeDtypeStruct((rows, HEAD_DIM), F32),
        jax.ShapeDtypeStruct((rows, HEAD_DIM), F32),
        jax.ShapeDtypeStruct((rows, HEAD_DIM), BF16),
        jax.ShapeDtypeStruct((rows // KEY_TILE, HEAD_DIM, KEY_TILE), BF16),
        jax.ShapeDtypeStruct((rows, 512), BF16),
        jax.ShapeDtypeStruct((rows, LANES), F32),
        jax.ShapeDtypeStruct((rows, HEAD_DIM_I), BF16),
        jax.ShapeDtypeStruct((rows, SHIFT_COLS), F32),
    ]
    out_specs = [row_spec(1024), row_spec(HEAD_DIM), row_spec(HEAD_DIM), row_spec(HEAD_DIM),
                 pl.BlockSpec((tm // KEY_TILE, HEAD_DIM, KEY_TILE), lambda i: (i, 0, 0)),
                 row_spec(512), row_spec(LANES), row_spec(HEAD_DIM_I), row_spec(SHIFT_COLS)]
    in_specs = [row_spec(D_MODEL),
                pl.BlockSpec((tm, 512), lambda i: (i % tab_blocks, 0)),
                _const_spec((1, D_MODEL)), _const_spec((1, D_MODEL)),
                _const_spec((D_MODEL, 1280)), _const_spec((D_MODEL, 640)), _const_spec((D_MODEL, SHIFT_COLS)),
                _const_spec((1, LANES)), _const_spec((1, LANES))]
    return pl.pallas_call(
        _proj_body, grid=(n,), in_specs=in_specs, out_specs=out_specs, out_shape=out_shape,
        compiler_params=pltpu.CompilerParams(dimension_semantics=("arbitrary",), vmem_limit_bytes=VMEM_LIMIT),
        name="proj",
    )(x, tab, w["ln0_g"], w["ln0_b"], w["wa"], w["wi"], w["wr"], w["ikg"], w["ikb"])


def _dsa_body(q_ref, qi_ref, kiwi_ref, kbf_ref, kibf_ref, vt_ref, o_ref,
              key_ref, bias_ref, m_ref, l_ref, acc_ref, *, causal, n_sel, n_keys):
    i = pl.program_id(1)
    if causal:
        lim_lo = KEY_TILE + i * (2 * CHUNK) + CHUNK
        lim_hi = lim_lo + CHUNK
    else:
        lim_lo = KEY_TILE + n_keys
        lim_hi = lim_lo
    n_score = (lim_hi + SCORE_ROWS - 1) // SCORE_ROWS

    qit = qi_ref[...].astype(F32).T.astype(BF16)
    wt = kiwi_ref[...].T
    lane = lax.broadcasted_iota(I32, (1, LANES), 1)

    def score_block(j, carry):
        r0 = pl.multiple_of(j * SCORE_ROWS, SCORE_ROWS)
        kib = kibf_ref[pl.ds(r0, SCORE_ROWS), :]
        sc = jnp.zeros((SCORE_ROWS, LANES), F32)
        for pr in range(N_HEADS_I // 2):
            rhs = jnp.concatenate([qit[(2 * pr) * HEAD_DIM_I:(2 * pr + 1) * HEAD_DIM_I, :],
                                   qit[(2 * pr + 1) * HEAD_DIM_I:(2 * pr + 2) * HEAD_DIM_I, :]], axis=1)
            s = _mm(kib, rhs)
            w0 = wt[HEAD_DIM_I + 2 * pr:HEAD_DIM_I + 2 * pr + 1, :]
            w1 = wt[HEAD_DIM_I + 2 * pr + 1:HEAD_DIM_I + 2 * pr + 2, :]
            sc = sc + jnp.maximum(s[:, :LANES], 0.0) * w0 + jnp.maximum(s[:, LANES:], 0.0) * w1
        sc = sc + 0.0
        bits = lax.bitcast_convert_type(sc, I32)
        key = bits ^ ((bits >> 31) & 0x7FFFFFFF)
        row = r0 + lax.broadcasted_iota(I32, (SCORE_ROWS, 1), 0)
        adm = (row < N_META) | ((row >= KEY_TILE) & (row < lim_lo)) | ((row >= lim_lo) & (row < lim_hi) & (lane >= CHUNK))
        key_ref[pl.ds(r0, SCORE_ROWS), :] = jnp.where(adm, key, INT_MIN)
        return carry

    lax.fori_loop(0, n_score, score_block, 0)
    n_tiles = n_score * (SCORE_ROWS // KEY_TILE)

    def count(pred):
        def body(j, acc):
            r0 = pl.multiple_of(j * KEY_TILE, KEY_TILE)
            m = pred(key_ref[pl.ds(r0, KEY_TILE), :], r0).astype(I32)
            return acc + m.reshape(KEY_TILE // 8, 8, LANES).sum(axis=0)
        acc = lax.fori_loop(0, n_tiles, body, jnp.zeros((8, LANES), I32))
        return acc.sum(axis=0, keepdims=True)

    def bit_step(b, t):
        cand = t + lax.shift_left(jnp.int32(1), 31 - b)
        cnt = count(lambda tile, r0: tile >= cand)
        return jnp.where(cnt >= n_sel, cand, t)

    thr = lax.fori_loop(0, 32, bit_step, jnp.full((1, LANES), INT_MIN, I32))

    cnt_gt = count(lambda tile, r0: tile > thr)
    cnt_eq = count(lambda tile, r0: tile == thr)
    need = n_sel - cnt_gt
    tied = (cnt_eq > need) & (thr > INT_MIN)
    big = jnp.full((1, LANES), 2 ** 30, I32)

    def tie_search():
        def rbit(b, r):
            cand = r + lax.shift_left(jnp.int32(1), 14 - b)
            cnt = count(lambda tile, r0: (tile == thr) & ((r0 + lax.broadcasted_iota(I32, (KEY_TILE, 1), 0)) < cand))
            return jnp.where(cnt <= need, cand, r)
        r = lax.fori_loop(0, 15, rbit, jnp.zeros((1, LANES), I32))
        return jnp.where(tied, r, big)

    row_lim = lax.cond(jnp.max(tied.astype(I32)) > 0, tie_search, lambda: big)
    eq_ok = thr > INT_MIN

    def bias_block(j, carry):
        r0 = pl.multiple_of(j * KEY_TILE, KEY_TILE)
        tile = key_ref[pl.ds(r0, KEY_TILE), :]
        row = r0 + lax.broadcasted_iota(I32, (KEY_TILE, 1), 0)
        sel = (tile > thr) | ((tile == thr) & (row < row_lim) & eq_ok)
        bias_ref[pl.ds(r0, KEY_TILE), :] = jnp.where(sel, 0.0, NEG)
        return carry

    lax.fori_loop(0, n_tiles, bias_block, 0)

    qt = q_ref[...].astype(F32).T.astype(BF16)
    m_ref[...] = jnp.full(m_ref.shape, NEG, F32)
    l_ref[...] = jnp.zeros(l_ref.shape, F32)
    acc_ref[...] = jnp.zeros(acc_ref.shape, F32)
    tiles_per_step = ATT_ROWS // KEY_TILE

    def att_block(j, carry):
        r0 = pl.multiple_of(j * ATT_ROWS, ATT_ROWS)
        kb = kbf_ref[pl.ds(r0, ATT_ROWS), :]
        vtb = jnp.concatenate([vt_ref[j * tiles_per_step + u] for u in range(tiles_per_step)], axis=1)
        bias = bias_ref[pl.ds(r0, ATT_ROWS), :]
        for pr in range(N_HEADS_A // 2):
            rhs = jnp.concatenate([qt[(2 * pr) * HEAD_DIM:(2 * pr + 1) * HEAD_DIM, :],
                                   qt[(2 * pr + 1) * HEAD_DIM:(2 * pr + 2) * HEAD_DIM, :]], axis=1)
            s2 = _mm(kb, rhs)
            for u in range(2):
                h = 2 * pr + u
                s = s2[:, u * LANES:(u + 1) * LANES] + bias
                m_old = m_ref[h:h + 1, :]
                m_new = jnp.maximum(m_old, jnp.max(s, axis=0, keepdims=True))
                alpha = jnp.exp(m_old - m_new)
                p = jnp.exp(s - m_new)
                l_ref[h:h + 1, :] = alpha * l_ref[h:h + 1, :] + jnp.sum(p, axis=0, keepdims=True)
                acc_ref[h * HEAD_DIM:(h + 1) * HEAD_DIM, :] = (
                    alpha * acc_ref[h * HEAD_DIM:(h + 1) * HEAD_DIM, :] + _mm(vtb, p.astype(BF16)))
                m_ref[h:h + 1, :] = m_new
        return carry

    lax.fori_loop(0, n_score * (SCORE_ROWS // ATT_ROWS), att_block, 0)
    for h in range(N_HEADS_A):
        inv = 1.0 / l_ref[h:h + 1, :]
        acc_ref[h * HEAD_DIM:(h + 1) * HEAD_DIM, :] = acc_ref[h * HEAD_DIM:(h + 1) * HEAD_DIM, :] * inv
    o_ref[...] = acc_ref[...].T.astype(BF16)


def _dsa(q, qi, kiwi, kbf, kibf, vt, *, n_batch, n_qblk, causal, n_sel, n_keys):
    sx = kbf.shape[1]
    qmap = lambda b, i: (b * n_qblk + i, 0)
    body = functools.partial(_dsa_body, causal=causal, n_sel=n_sel, n_keys=n_keys)
    return pl.pallas_call(
        body, grid=(n_batch, n_qblk),
        in_specs=[pl.BlockSpec((LANES, 1024), qmap), pl.BlockSpec((LANES, 512), qmap), pl.BlockSpec((LANES, LANES), qmap),
                  pl.BlockSpec((None, sx, HEAD_DIM), lambda b, i: (b, 0, 0)),
                  pl.BlockSpec((None, sx, HEAD_DIM_I), lambda b, i: (b, 0, 0)),
                  pl.BlockSpec((None, sx // KEY_TILE, HEAD_DIM, KEY_TILE), lambda b, i: (b, 0, 0, 0))],
        out_specs=pl.BlockSpec((LANES, 1024), qmap),
        out_shape=jax.ShapeDtypeStruct((n_batch * n_qblk * LANES, 1024), BF16),
        scratch_shapes=[pltpu.VMEM((sx, LANES), I32), pltpu.VMEM((sx, LANES), F32),
                        pltpu.VMEM((N_HEADS_A, LANES), F32), pltpu.VMEM((N_HEADS_A, LANES), F32),
                        pltpu.VMEM((N_HEADS_A * HEAD_DIM, LANES), F32)],
        compiler_params=pltpu.CompilerParams(dimension_semantics=("arbitrary", "arbitrary"), vmem_limit_bytes=VMEM_LIMIT),
        name="dsa",
    )(q, qi, kiwi, kbf, kibf, vt)


def _rwkv_body(rw_ref, shift0_ref, h0_ref, mu_ref, w0_ref, w2_ref, a0_ref, a2_ref, g2_ref,
               kk_ref, ka_ref, rk_ref, gng_ref, gnb_ref,
               out_ref, hfin_ref,
               prev_ref, st_ref, op_ref, y_ref, *, chunk):
    c = pl.program_id(1)
    n_c = pl.num_programs(1)
    C = chunk

    @pl.when(c == 0)
    def _():
        prev_ref[...] = shift0_ref[...]
        st_ref[...] = h0_ref[...]

    x = rw_ref[...]
    rowi = lax.broadcasted_iota(I32, (C, 1), 0)
    prev = jnp.where(rowi == 0, prev_ref[...], pltpu.roll(x, 1, 0))
    prev_ref[...] = x[C - 1:C, :]
    xs = x + (prev - x) * mu_ref[...]
    r = xs[:, 0:1024]
    k = xs[:, 1024:2048]
    v = xs[:, 2048:3072]
    lw = xs[:, 3072:3072 + LORA_W]
    la = xs[:, 3072 + LORA_W:3072 + LORA_W + LORA_A]
    lg = xs[:, 3072 + LORA_W + LORA_A:SHIFT_COLS]
    wx = w0_ref[...] + _dot_hi(jnp.tanh(lw), w2_ref[...])
    w_log = -jax.nn.softplus(-wx) - 0.5
    ld = -jnp.exp(w_log)
    a = jax.nn.sigmoid(a0_ref[...] + _dot_hi(la, a2_ref[...]))
    g = _dot_hi(jax.nn.sigmoid(lg), g2_ref[...])
    tri = (lax.broadcasted_iota(I32, (C, C), 0) >= lax.broadcasted_iota(I32, (C, C), 1)).astype(BF16)
    cl = _cumsum_rows(ld, tri)

    for p in range(RW_PAIRS):
        sl = slice(p * LANES, (p + 1) * LANES)
        op_ref[0, p] = r[:, sl]
        op_ref[1, p] = k[:, sl]
        op_ref[2, p] = v[:, sl]
        op_ref[3, p] = a[:, sl]
        op_ref[4, p] = ld[:, sl]
        op_ref[5, p] = cl[:, sl]
        op_ref[6, p] = g[:, sl]

    lane = lax.broadcasted_iota(I32, (1, LANES), 1)
    lane_h0 = lane < RW_HEAD
    r2 = lax.broadcasted_iota(I32, (2 * C, 2 * C), 0)
    c2 = lax.broadcasted_iota(I32, (2 * C, 2 * C), 1)
    same_head = (r2 < C) == (c2 < C)
    strict = same_head & ((r2 % C) > (c2 % C))
    incl = same_head & ((r2 % C) >= (c2 % C))
    rl = lax.broadcasted_iota(I32, (LANES, LANES), 0)
    cl_ = lax.broadcasted_iota(I32, (LANES, LANES), 1)
    blockdiag = (rl < RW_HEAD) == (cl_ < RW_HEAD)
    eye = rl == cl_
    ones_bd = blockdiag.astype(BF16)

    def stack(t):
        return jnp.concatenate([jnp.where(lane_h0, t, 0.0), jnp.where(lane_h0, 0.0, t)], axis=0)

    def head_sum(t):
        return _dot_exact_rhs(t, ones_bd)

    def pair_step(p, carry):
        rp, kp, vp, ap, ldp, clp, gp = (op_ref[q, p] for q in range(7))
        kkw, kaw, rkw, gng, gnb = kk_ref[p], ka_ref[p], rk_ref[p], gng_ref[p], gnb_ref[p]
        kk = kp * kkw
        kk = kk / jnp.maximum(jnp.sqrt(head_sum(kk * kk)), 1e-12)
        k2 = kp * (1.0 + (ap - 1.0) * kaw)
        e_in = jnp.exp(clp)
        e_ex = jnp.exp(clp - ldp)
        e_neg = jnp.exp(-clp)
        e_c = e_in[C - 1:C, :]
        at = -kk * e_ex
        rt = rp * e_in
        bt = kk * ap * e_neg
        kt = k2 * e_neg
        hbd = st_ref[p]

        a_s, r_s, b_s, k_s, v_s = stack(at), stack(rt), stack(bt), stack(kt), stack(vp)
        ar = jnp.concatenate([a_s, r_s], axis=0)
        arh = _dot_hi(ar, hbd)
        if (2 * C) % LANES == 0:
            m = _dot_hi(ar, jnp.concatenate([b_s, k_s], axis=0), _NT)
            m_ab, m_ak = m[:2 * C, :2 * C], m[:2 * C, 2 * C:]
            m_rb, m_rk = m[2 * C:, :2 * C], m[2 * C:, 2 * C:]
        else:
            m_ab, m_ak = _dot_hi(a_s, b_s, _NT), _dot_hi(a_s, k_s, _NT)
            m_rb, m_rk = _dot_hi(r_s, b_s, _NT), _dot_hi(r_s, k_s, _NT)
        m_ab = jnp.where(strict, m_ab, 0.0)
        m_ak = jnp.where(strict, m_ak, 0.0)
        m_rb = jnp.where(incl, m_rb, 0.0)
        m_rk = jnp.where(incl, m_rk, 0.0)
        u_s = arh[:2 * C] + _dot_hi(m_ak, v_s)
        xpow = m_ab
        n = 1
        while n < C:
            u_s = u_s + _dot_hi(xpow, u_s)
            n *= 2
            if n < C:
                xpow = _dot_hi(xpow, xpow)
        if (2 * C) % LANES == 0:
            y_s = arh[2 * C:] + _dot_hi(jnp.concatenate([m_rb, m_rk], axis=1), jnp.concatenate([u_s, v_s], axis=0))
        else:
            y_s = arh[2 * C:] + (_dot_hi(m_rb, u_s) + _dot_hi(m_rk, v_s))
        y = y_s[:C] + y_s[C:]
        u = u_s[:C] + u_s[C:]
        dmat = jnp.where(eye, jnp.broadcast_to(e_c, (LANES, LANES)), 0.0)
        lhs = jnp.concatenate([bt * e_c, kt * e_c, dmat], axis=0)
        rhs = jnp.concatenate([u, vp, hbd], axis=0)
        st_ref[p] = jnp.where(blockdiag, _dot_hi(lhs, rhs, _TN), 0.0)

        mu_y = head_sum(y) * (1.0 / RW_HEAD)
        dy = y - mu_y
        var = head_sum(dy * dy) * (1.0 / RW_HEAD)
        yn = dy * lax.rsqrt(var + RW_GN_EPS) * gng + gnb
        bonus = head_sum(rp * k2 * rkw) * vp
        y_ref[p] = ((yn + bonus) * gp).astype(BF16)
        return carry

    lax.fori_loop(0, RW_PAIRS, pair_step, 0)
    out_ref[...] = jnp.concatenate([y_ref[p] for p in range(RW_PAIRS)], axis=1)

    @pl.when(c == n_c - 1)
    def _():
        hfin_ref[...] = st_ref[...]


def _cumsum_rows(x, tri_bf16):
    xh, xm, xl = _split3(x)
    return _mm(tri_bf16, xh) + (_mm(tri_bf16, xm) + _mm(tri_bf16, xl))


def _rwkv(rw, shift0, h0, w, *, n_batch, n_chunks, chunk, row_blocks_per_batch):
    body = functools.partial(_rwkv_body, chunk=chunk)
    pp = lambda: _const_spec((RW_PAIRS, 1, LANES))
    return pl.pallas_call(
        body, grid=(n_batch, n_chunks),
        in_specs=[pl.BlockSpec((chunk, SHIFT_COLS), lambda b, c: (b * row_blocks_per_batch + c, 0)),
                  pl.BlockSpec((None, 1, SHIFT_COLS), lambda b, c: (b, 0, 0)),
                  pl.BlockSpec((None, RW_PAIRS, LANES, LANES), lambda b, c: (b, 0, 0, 0)),
                  _const_spec((1, SHIFT_COLS)), _const_spec((1, D_MODEL)), _const_spec((LORA_W, D_MODEL)),
                  _const_spec((1, D_MODEL)), _const_spec((LORA_A, D_MODEL)), _const_spec((LORA_G, D_MODEL)),
                  pp(), pp(), pp(), pp(), pp()],
        out_specs=[pl.BlockSpec((chunk, D_MODEL), lambda b, c: (b * n_chunks + c, 0)),
                   pl.BlockSpec((None, RW_PAIRS, LANES, LANES), lambda b, c: (b, 0, 0, 0))],
        out_shape=[jax.ShapeDtypeStruct((n_batch * n_chunks * chunk, D_MODEL), BF16),
                   jax.ShapeDtypeStruct((n_batch, RW_PAIRS, LANES, LANES), F32)],
        scratch_shapes=[pltpu.VMEM((1, SHIFT_COLS), F32), pltpu.VMEM((RW_PAIRS, LANES, LANES), F32),
                        pltpu.VMEM((7, RW_PAIRS, chunk, LANES), F32), pltpu.VMEM((RW_PAIRS, chunk, LANES), BF16)],
        compiler_params=pltpu.CompilerParams(dimension_semantics=("arbitrary", "arbitrary"), vmem_limit_bytes=VMEM_LIMIT),
        name="rwkv",
    )(rw, shift0, h0, w["rw_mu"], w["rw_w0"], w["rw_w2"], w["rw_a0"], w["rw_a2"], w["rw_g2"],
      w["rw_k_k"], w["rw_k_a"], w["rw_r_k"], w["rw_gn_g"], w["rw_gn_b"])


def _merge_body(x_ref, oa_ref, orw_ref, g0_ref, b0_ref, wg_ref, woa_ref, wor_ref, wout_ref, g1_ref, b1_ref, x1_ref):
    h = _layernorm(x_ref[...], g0_ref[...], b0_ref[...])
    gates = jax.nn.sigmoid(_mm(h.astype(BF16), wg_ref[...]))
    o_attn = _mm(oa_ref[...], woa_ref[...])
    o_rw = _mm(orw_ref[...], wor_ref[...])
    mixed = _mm((gates[:, :D_MODEL] * o_attn + gates[:, D_MODEL:] * o_rw).astype(BF16), wout_ref[...])
    x1_ref[...] = _layernorm(ALPHA * h + mixed, g1_ref[...], b1_ref[...])


def _merge(x, oa, orw, w, tm):
    rows = x.shape[0]
    spec = lambda: pl.BlockSpec((tm, D_MODEL), lambda i: (i, 0))
    vec = lambda: _const_spec((1, D_MODEL))
    sq = lambda: _const_spec((D_MODEL, D_MODEL))
    return pl.pallas_call(
        _merge_body, grid=(rows // tm,),
        in_specs=[spec(), spec(), spec(), vec(), vec(), _const_spec((D_MODEL, 2 * D_MODEL)), sq(), sq(), sq(), vec(), vec()],
        out_specs=spec(), out_shape=jax.ShapeDtypeStruct((rows, D_MODEL), F32),
        compiler_params=pltpu.CompilerParams(dimension_semantics=("arbitrary",), vmem_limit_bytes=VMEM_LIMIT),
        name="merge",
    )(x, oa, orw, w["ln0_g"], w["ln0_b"], w["wg"], w["w_o_attn"], w["w_o_rwkv"], w["w_out"], w["ln1_g"], w["ln1_b"])


def _ffn_body(x1_ref, wgate_ref, wup_ref, wdown_ref, g2_ref, b2_ref, y_ref):
    x1 = x1_ref[...]
    xb = x1.astype(BF16)
    hidden = jax.nn.silu(_mm(xb, wgate_ref[...])) * _mm(xb, wup_ref[...])
    ffn = _mm(hidden.astype(BF16), wdown_ref[...])
    y_ref[...] = _layernorm(ALPHA * x1 + ffn, g2_ref[...], b2_ref[...])


def _ffn(x1, w, tm):
    rows = x1.shape[0]
    spec = lambda: pl.BlockSpec((tm, D_MODEL), lambda i: (i, 0))
    return pl.pallas_call(
        _ffn_body, grid=(rows // tm,),
        in_specs=[spec(), _const_spec((D_MODEL, D_FF)), _const_spec((D_MODEL, D_FF)), _const_spec((D_FF, D_MODEL)),
                  _const_spec((1, D_MODEL)), _const_spec((1, D_MODEL))],
        out_specs=spec(), out_shape=jax.ShapeDtypeStruct((rows, D_MODEL), F32),
        compiler_params=pltpu.CompilerParams(dimension_semantics=("arbitrary",), vmem_limit_bytes=VMEM_LIMIT),
        name="ffn",
    )(x1, w["ffn_w_gate"], w["ffn_w_up"], w["ffn_w_down"], w["ln2_g"], w["ln2_b"])


def _rope_table(pos):
    pos = pos.astype(F32)[:, None]

    def one(rot, period):
        half = rot // 2
        inv = ROPE_THETA ** (-jnp.arange(0, rot, 2, dtype=F32) / rot)
        ang = pos * inv[None]
        cos, sin = jnp.cos(ang), jnp.sin(ang)
        t = pos.shape[0]
        ones = jnp.ones((t, period - rot), F32)
        zeros = jnp.zeros((t, period - rot), F32)
        c = jnp.concatenate([cos, cos, ones], axis=1)
        s = jnp.concatenate([-sin, sin, zeros], axis=1)
        reps = LANES // period
        return jnp.tile(c, (1, reps)), jnp.tile(s, (1, reps))

    ca, sa = one(ROPE_DIM_A, HEAD_DIM)
    ci, si = one(ROPE_DIM_I, HEAD_DIM_I)
    return jnp.concatenate([ca, sa, ci, si], axis=1)


def _pairs(vec):
    return vec.reshape(RW_PAIRS, 1, LANES).astype(F32)


def _state_to_pairs(s):
    b = s.shape[0]
    ht = jnp.swapaxes(s, -1, -2).reshape(b, RW_PAIRS, 2, RW_HEAD, RW_HEAD)
    z = jnp.zeros((b, RW_PAIRS, RW_HEAD, RW_HEAD), s.dtype)
    top = jnp.concatenate([ht[:, :, 0], z], axis=-1)
    bot = jnp.concatenate([z, ht[:, :, 1]], axis=-1)
    return jnp.concatenate([top, bot], axis=-2)


def _pairs_to_state(h):
    b = h.shape[0]
    h0 = h[:, :, :RW_HEAD, :RW_HEAD]
    h1 = h[:, :, RW_HEAD:, RW_HEAD:]
    ht = jnp.stack([h0, h1], axis=2).reshape(b, RW_HEADS, RW_HEAD, RW_HEAD)
    return jnp.swapaxes(ht, -1, -2)


def _pad_rows(a, rows, axis=1):
    pad = [(0, 0)] * a.ndim
    pad[axis] = (0, rows - a.shape[axis])
    return jnp.pad(a, pad)


def kernel(x_prompt, x_sample, cache_k, cache_v, cache_idx_k, state_wkv, state_shift, meta_tokens, ln0_g, ln0_b, w_in, idx_k_ln_g, idx_k_ln_b, rw_mu, rw_w0, rw_w2, rw_a0, rw_a2, rw_g2, rw_k_k, rw_k_a, rw_r_k, rw_gn_g, rw_gn_b, w_o_attn, w_o_rwkv, w_out, ln1_g, ln1_b, ffn_w_gate, ffn_w_up, ffn_w_down, ln2_g, ln2_b):
    B, S, _ = x_prompt.shape
    Bs, Ts, _ = x_sample.shape
    P = cache_k.shape[2] - N_META
    depth = w_in.shape[0]
    assert depth == 1 and S % (2 * CHUNK) == 0 and P % KEY_TILE == 0 and Ts == CHUNK
    l = 0
    row = lambda v: v.reshape(1, -1).astype(F32)
    win = w_in[l]
    wi_pack = jnp.concatenate([win[:, _QI0:_G0], jnp.zeros((D_MODEL, 640 - (_G0 - _QI0)), F32)], axis=1)
    lane_pad = lambda v: jnp.concatenate([v, jnp.zeros((LANES - HEAD_DIM_I,), F32)]).reshape(1, LANES)
    w = {
        "ln0_g": row(ln0_g), "ln0_b": row(ln0_b),
        "wa": win[:, _Q0:_QI0].astype(BF16), "wi": wi_pack.astype(BF16), "wr": win[:, _RW0:].astype(BF16),
        "wg": win[:, _G0:_RW0].astype(BF16),
        "ikg": lane_pad(idx_k_ln_g[l]), "ikb": lane_pad(idx_k_ln_b[l]),
        "rw_mu": row(rw_mu[l]), "rw_w0": row(rw_w0[l]), "rw_w2": rw_w2[l], "rw_a0": row(rw_a0[l]),
        "rw_a2": rw_a2[l], "rw_g2": rw_g2[l],
        "rw_k_k": _pairs(rw_k_k[l]), "rw_k_a": _pairs(rw_k_a[l]), "rw_r_k": _pairs(rw_r_k[l]),
        "rw_gn_g": _pairs(rw_gn_g[l]), "rw_gn_b": _pairs(rw_gn_b[l]),
        "w_o_attn": w_o_attn[l].astype(BF16), "w_o_rwkv": w_o_rwkv[l].astype(BF16), "w_out": w_out[l].astype(BF16),
        "ln1_g": row(ln1_g[l]), "ln1_b": row(ln1_b[l]),
        "ffn_w_gate": ffn_w_gate[l].astype(BF16), "ffn_w_up": ffn_w_up[l].astype(BF16),
        "ffn_w_down": ffn_w_down[l].astype(BF16), "ln2_g": row(ln2_g[l]), "ln2_b": row(ln2_b[l]),
    }

    tm = 256
    x_meta = _pad_rows(meta_tokens.astype(F32), KEY_TILE, axis=0)
    m_q, m_k, m_v, m_kbf, m_vt, m_qi, m_kiwi, m_kibf, m_rw = _proj(
        x_meta, _rope_table(jnp.arange(KEY_TILE)), 1, KEY_TILE, w)
    xf = x_prompt.reshape(B * S, D_MODEL)
    f_q, f_k, f_v, f_kbf, f_vt, f_qi, f_kiwi, f_kibf, f_rw = _proj(
        xf, _rope_table(N_META + jnp.arange(S)), S // tm, tm, w)
    xs_pad = _pad_rows(x_sample, LANES, axis=1).reshape(Bs * LANES, D_MODEL)
    s_q, s_k, s_v, s_kbf, s_vt, s_qi, s_kiwi, s_kibf, s_rw = _proj(
        xs_pad, _rope_table(N_META + P + jnp.arange(LANES)), 1, LANES, w)

    def key_rows(meta_tile, frames, total):
        b = frames.shape[0]
        mt = jnp.broadcast_to(meta_tile[None], (b,) + meta_tile.shape)
        return _pad_rows(jnp.concatenate([mt, frames], axis=1), total, axis=1)

    sx_p = -(-(KEY_TILE + S) // SCORE_ROWS) * SCORE_ROWS
    o_attn_p = _dsa(
        f_q, f_qi, f_kiwi,
        key_rows(m_kbf, f_kbf.reshape(B, S, HEAD_DIM), sx_p),
        key_rows(m_kibf, f_kibf.reshape(B, S, HEAD_DIM_I), sx_p),
        key_rows(m_vt, f_vt.reshape(B, S // KEY_TILE, HEAD_DIM, KEY_TILE), sx_p // KEY_TILE),
        n_batch=B, n_qblk=S // LANES, causal=True, n_sel=min(TOPK_MAX, S // 4), n_keys=S)

    n_keys_s = P + Ts
    sx_s = -(-(KEY_TILE + P + LANES) // SCORE_ROWS) * SCORE_ROWS
    meta_rows = lambda c: _pad_rows(c[l][:, :N_META], KEY_TILE, axis=1)
    past_vt = jnp.swapaxes(cache_v[l][:, N_META:].reshape(Bs, P // KEY_TILE, KEY_TILE, HEAD_DIM), -1, -2)
    meta_vt = jnp.swapaxes(meta_rows(cache_v), -1, -2)[:, None]
    kbf_s = _pad_rows(jnp.concatenate([meta_rows(cache_k).astype(BF16), cache_k[l][:, N_META:].astype(BF16),
                                       s_kbf.reshape(Bs, LANES, HEAD_DIM)], axis=1), sx_s)
    kibf_s = _pad_rows(jnp.concatenate([meta_rows(cache_idx_k).astype(BF16), cache_idx_k[l][:, N_META:].astype(BF16),
                                        s_kibf.reshape(Bs, LANES, HEAD_DIM_I)], axis=1), sx_s)
    vt_s = _pad_rows(jnp.concatenate([meta_vt.astype(BF16), past_vt.astype(BF16),
                                      s_vt.reshape(Bs, 1, HEAD_DIM, KEY_TILE)], axis=1), sx_s // KEY_TILE)
    o_attn_s = _dsa(s_q, s_qi, s_kiwi, kbf_s, kibf_s, vt_s,
                    n_batch=Bs, n_qblk=1, causal=False, n_sel=min(TOPK_MAX, n_keys_s // 4), n_keys=n_keys_s)
    o_attn_s = o_attn_s.reshape(Bs, LANES, D_MODEL)[:, :Ts].reshape(Bs * Ts, D_MODEL)

    zero_shift = jnp.zeros((1, 1, SHIFT_COLS), F32)
    zero_state = jnp.zeros((1, RW_PAIRS, LANES, LANES), F32)
    _, h_meta = _rwkv(m_rw, zero_shift, zero_state, w, n_batch=1, n_chunks=1, chunk=N_META, row_blocks_per_batch=0)
    shift_meta = m_rw[N_META - 1:N_META].reshape(1, 1, SHIFT_COLS)
    o_rw_p, h_p = _rwkv(f_rw, jnp.broadcast_to(shift_meta, (B, 1, SHIFT_COLS)),
                        jnp.broadcast_to(h_meta, (B, RW_PAIRS, LANES, LANES)), w,
                        n_batch=B, n_chunks=S // CHUNK, chunk=CHUNK, row_blocks_per_batch=S // CHUNK)
    o_rw_s, h_s = _rwkv(s_rw, state_shift[l].reshape(Bs, 1, SHIFT_COLS), _state_to_pairs(state_wkv[l]), w,
                        n_batch=Bs, n_chunks=1, chunk=CHUNK, row_blocks_per_batch=LANES // CHUNK)

    y_p = _ffn(_merge(xf, o_attn_p, o_rw_p, w, tm), w, tm)
    xs_flat = x_sample.reshape(Bs * Ts, D_MODEL)
    tm_s = min(tm, Bs * Ts)
    y_s = _ffn(_merge(xs_flat, o_attn_s, o_rw_s, w, tm_s), w, tm_s)

    def with_meta(meta_rows_, frames, width):
        mt = jnp.broadcast_to(meta_rows_[None, :N_META, :width], (B, N_META, width))
        return jnp.concatenate([mt, frames.reshape(B, S, -1)[:, :, :width]], axis=1)[None]

    take = lambda a, width: a.reshape(Bs, LANES, -1)[:, :Ts, :width][None]
    return (
        y_p.reshape(B, S, D_MODEL),
        y_s.reshape(Bs, Ts, D_MODEL),
        with_meta(m_k, f_k, HEAD_DIM),
        with_meta(m_v, f_v, HEAD_DIM),
        with_meta(m_kiwi, f_kiwi, HEAD_DIM_I),
        _pairs_to_state(h_p)[None],
        f_rw.reshape(B, S, SHIFT_COLS)[:, -1][None],
        take(s_k, HEAD_DIM),
        take(s_v, HEAD_DIM),
        take(s_kiwi, HEAD_DIM_I),
        _pairs_to_state(h_s)[None],
        s_rw.reshape(Bs, LANES, SHIFT_COLS)[:, Ts - 1][None],
    )
```

```python
import functools

import numpy as np
import jax
import jax.numpy as jnp
from jax import lax
from jax.experimental import pallas as pl
from jax.experimental.pallas import tpu as pltpu

F32 = jnp.float32
BF16 = jnp.bfloat16
I32 = jnp.int32

D_MODEL = 1024
N_META = 16
CHUNK = 64
N_HEADS_A = 8
HEAD_DIM = 128
ROPE_DIM_A = HEAD_DIM // 4
N_HEADS_I = 8
HEAD_DIM_I = 64
ROPE_DIM_I = HEAD_DIM_I // 4
TOPK_MAX = 256
ROPE_THETA = 500000.0
RW_HEAD = 64
RW_HEADS = 16
RW_PAIRS = RW_HEADS // 2
LORA_W = 64
LORA_A = 64
LORA_G = 128
RW_GN_EPS = 64e-5
D_FF = 2816
LN_EPS = 1e-5
ALPHA = 2.0 ** 0.25
NEG = -1e30
SHIFT_COLS = 3 * D_MODEL + LORA_W + LORA_A + LORA_G
_Q0, _K0, _V0, _QI0, _KI0, _WI0, _G0, _RW0 = 0, 1024, 1152, 1280, 1792, 1856, 1864, 3912
_IN_COLS = _RW0 + SHIFT_COLS

LANES = 128
KEY_TILE = 128
SCORE_ROWS = 512
ATT_ROWS = 256
INT_MIN = -2 ** 31
VMEM_LIMIT = 56 * 1024 * 1024

_NN = (((1,), (0,)), ((), ()))
_NT = (((1,), (1,)), ((), ()))
_TN = (((0,), (0,)), ((), ()))
_BNN = (((2,), (1,)), ((0,), (0,)))
_BNT = (((2,), (2,)), ((0,), (0,)))
_BTN = (((1,), (1,)), ((0,), (0,)))


def _mm(a, b, dn=_NN):
    return lax.dot_general(a, b, dn, preferred_element_type=F32)


def _split3(a):
    hi = a.astype(BF16)
    r1 = a - hi.astype(F32)
    mid = r1.astype(BF16)
    lo = (r1 - mid.astype(F32)).astype(BF16)
    return hi, mid, lo


def _dot_hi(a, b, dn=_NN):
    ah, am, _ = _split3(a)
    bh, bm, _ = _split3(b)
    return _mm(ah, bh, dn) + (_mm(ah, bm, dn) + _mm(am, bh, dn))


def _dot_exact_rhs(a, b_bf16, dn=_NN):
    ah = a.astype(BF16)
    am = (a - ah.astype(F32)).astype(BF16)
    return _mm(ah, b_bf16, dn) + _mm(am, b_bf16, dn)


def _layernorm(x, g, b):
    mu = jnp.mean(x, axis=-1, keepdims=True)
    d = x - mu
    var = jnp.mean(d * d, axis=-1, keepdims=True)
    return d * lax.rsqrt(var + LN_EPS) * g + b


def _const_spec(shape):
    nd = len(shape)
    return pl.BlockSpec(shape, lambda *_: (0,) * nd, pipeline_mode=pl.Buffered(1))


def _rope(x, cos, sin, period, half):
    width = x.shape[1]
    reps = width // LANES
    cc = jnp.concatenate([cos] * reps, axis=1) if reps > 1 else cos
    ss = jnp.concatenate([sin] * reps, axis=1) if reps > 1 else sin
    lane = lax.broadcasted_iota(I32, (1, width), 1) % period
    fwd = pltpu.roll(x, width - half, 1)
    bwd = pltpu.roll(x, half, 1)
    return x * cc + jnp.where(lane < half, fwd, bwd) * ss


def _proj_body(x_ref, tab_ref, g0_ref, b0_ref, wa_ref, wi_ref, wr_ref, ikg_ref, ikb_ref,
               q_ref, k_ref, v_ref, kbf_ref, vt_ref, qi_ref, kiwi_ref, kibf_ref, rw_ref):
    tm = x_ref.shape[0]
    hb = _layernorm(x_ref[...], g0_ref[...], b0_ref[...]).astype(BF16)
    tab = tab_ref[...]
    cos_a, sin_a = tab[:, 0:128], tab[:, 128:256]
    cos_i, sin_i = tab[:, 256:384], tab[:, 384:512]

    pa = _mm(hb, wa_ref[...])
    q = _rope(pa[:, :1024], cos_a, sin_a, HEAD_DIM, ROPE_DIM_A // 2)
    q_ref[...] = (q * HEAD_DIM ** -0.5).astype(BF16)
    k = _rope(pa[:, 1024:1152], cos_a, sin_a, HEAD_DIM, ROPE_DIM_A // 2)
    k_ref[...] = k
    kbf_ref[...] = k.astype(BF16)
    v = pa[:, 1152:1280]
    v_ref[...] = v
    for j in range(tm // KEY_TILE):
        vt_ref[j] = v[j * KEY_TILE:(j + 1) * KEY_TILE, :].T.astype(BF16)

    pi = _mm(hb, wi_ref[...])
    qi_ref[...] = _rope(pi[:, :512], cos_i, sin_i, HEAD_DIM_I, ROPE_DIM_I // 2).astype(BF16)
    kw = pi[:, 512:640]
    lane = lax.broadcasted_iota(I32, (1, LANES), 1)
    is_ki = lane < HEAD_DIM_I
    mu = jnp.sum(jnp.where(is_ki, kw, 0.0), axis=-1, keepdims=True) * (1.0 / HEAD_DIM_I)
    d = jnp.where(is_ki, kw - mu, 0.0)
    var = jnp.sum(d * d, axis=-1, keepdims=True) * (1.0 / HEAD_DIM_I)
    ki = d * lax.rsqrt(var + LN_EPS) * ikg_ref[...] + ikb_ref[...]
    ki = _rope(ki, cos_i, sin_i, HEAD_DIM_I, ROPE_DIM_I // 2)
    wi_scale = (N_HEADS_I ** -0.5) * (HEAD_DIM_I ** -0.5)
    kiwi_ref[...] = jnp.where(is_ki, ki, jnp.where(lane < HEAD_DIM_I + N_HEADS_I, kw * wi_scale, 0.0))
    kibf_ref[...] = ki[:, :HEAD_DIM_I].astype(BF16)

    rw_ref[...] = _mm(hb, wr_ref[...])


def _proj(x, tab, tab_blocks, tm, w):
    rows = x.shape[0]
    n = rows // tm
    row_spec = lambda width: pl.BlockSpec((tm, width), lambda i: (i, 0))
    out_shape = [
        jax.ShapeDtypeStruct((rows, 1024), BF16),
        jax.ShapeDtypeStruct((rows, HEAD_DIM), F32),
        jax.ShapeDtypeStruct((rows, HEAD_DIM), F32),
        jax.ShapeDtypeStruct((rows, HEAD_DIM), BF16),
        jax.ShapeDtypeStruct((rows // KEY_TILE, HEAD_DIM, KEY_TILE), BF16),
        jax.ShapeDtypeStruct((rows, 512), BF16),
        jax.ShapeDtypeStruct((rows, LANES), F32),
        jax.ShapeDtypeStruct((rows, HEAD_DIM_I), BF16),
        jax.ShapeDtypeStruct((rows, SHIFT_COLS), F32),
    ]
    out_specs = [row_spec(1024), row_spec(HEAD_DIM), row_spec(HEAD_DIM), row_spec(HEAD_DIM),
                 pl.BlockSpec((tm // KEY_TILE, HEAD_DIM, KEY_TILE), lambda i: (i, 0, 0)),
                 row_spec(512), row_spec(LANES), row_spec(HEAD_DIM_I), row_spec(SHIFT_COLS)]
    in_specs = [row_spec(D_MODEL),
                pl.BlockSpec((tm, 512), lambda i: (i % tab_blocks, 0)),
                _const_spec((1, D_MODEL)), _const_spec((1, D_MODEL)),
                _const_spec((D_MODEL, 1280)), _const_spec((D_MODEL, 640)), _const_spec((D_MODEL, SHIFT_COLS)),
                _const_spec((1, LANES)), _const_spec((1, LANES))]
    return pl.pallas_call(
        _proj_body, grid=(n,), in_specs=in_specs, out_specs=out_specs, out_shape=out_shape,
        compiler_params=pltpu.CompilerParams(dimension_semantics=("arbitrary",), vmem_limit_bytes=VMEM_LIMIT),
        name="proj",
    )(x, tab, w["ln0_g"], w["ln0_b"], w["wa"], w["wi"], w["wr"], w["ikg"], w["ikb"])


def _dsa_body(q_ref, qi_ref, kiwi_ref, kbf_ref, kibf_ref, vt_ref, o_ref,
              key_ref, bias_ref, m_ref, l_ref, acc_ref, s_ref, *, causal, n_sel, n_keys):
    i = pl.program_id(1)
    if causal:
        lim_lo = KEY_TILE + i * (2 * CHUNK) + CHUNK
        lim_hi = lim_lo + CHUNK
    else:
        lim_lo = KEY_TILE + n_keys
        lim_hi = lim_lo
    n_score = (lim_hi + SCORE_ROWS - 1) // SCORE_ROWS

    heads_on_lanes = lambda t, n, rows: jnp.concatenate([t[h * rows:(h + 1) * rows, :] for h in range(n)], axis=1)
    qi_all = heads_on_lanes(qi_ref[...].astype(F32).T.astype(BF16), N_HEADS_I, HEAD_DIM_I)
    w_all = heads_on_lanes(kiwi_ref[...].T[HEAD_DIM_I:HEAD_DIM_I + N_HEADS_I, :], N_HEADS_I, 1)
    lane = lax.broadcasted_iota(I32, (1, LANES), 1)

    def score_block(j, carry):
        r0 = pl.multiple_of(j * SCORE_ROWS, SCORE_ROWS)
        s = jnp.maximum(_mm(kibf_ref[pl.ds(r0, SCORE_ROWS), :], qi_all), 0.0) * w_all
        sc = s[:, :LANES]
        for h in range(1, N_HEADS_I):
            sc = sc + s[:, h * LANES:(h + 1) * LANES]
        sc = sc + 0.0
        bits = lax.bitcast_convert_type(sc, I32)
        key = bits ^ ((bits >> 31) & 0x7FFFFFFF)
        row = r0 + lax.broadcasted_iota(I32, (SCORE_ROWS, 1), 0)
        adm = (row < N_META) | ((row >= KEY_TILE) & (row < lim_lo)) | ((row >= lim_lo) & (row < lim_hi) & (lane >= CHUNK))
        key_ref[pl.ds(r0, SCORE_ROWS), :] = jnp.where(adm, key, INT_MIN)
        return carry

    lax.fori_loop(0, n_score, score_block, 0)
    tiles_per_blk = SCORE_ROWS // KEY_TILE
    n_tiles = n_score * tiles_per_blk

    def count(pred):
        def body(j, acc):
            for u in range(tiles_per_blk):
                r0 = pl.multiple_of(j * SCORE_ROWS + u * KEY_TILE, KEY_TILE)
                m = pred(key_ref[pl.ds(r0, KEY_TILE), :], r0)
                acc = jnp.where(m.reshape(KEY_TILE // 8, 8, LANES), acc + 1, acc)
            return acc
        acc = lax.fori_loop(0, n_score, body, jnp.zeros((KEY_TILE // 8, 8, LANES), I32))
        return acc.sum(axis=0).sum(axis=0, keepdims=True)

    def bit_step(b, st):
        t, cnt_t = st
        cand = t + lax.shift_left(jnp.int32(1), 31 - b)
        cnt = count(lambda tile, r0: tile >= cand)
        take = cnt >= n_sel
        return jnp.where(take, cand, t), jnp.where(take, cnt, cnt_t)

    all_rows = jnp.zeros((1, LANES), I32) + n_score * SCORE_ROWS
    thr, cnt_thr = lax.fori_loop(0, 32, bit_step, (jnp.full((1, LANES), INT_MIN, I32), all_rows))
    big = jnp.full((1, LANES), 2 ** 30, I32)

    def tie_path():
        cnt_gt = count(lambda tile, r0: tile > thr)
        need = n_sel - cnt_gt
        tied = (cnt_thr - cnt_gt > need) & (thr > INT_MIN)

        def tie_search():
            def rbit(b, r):
                cand = r + lax.shift_left(jnp.int32(1), 14 - b)
                cnt = count(lambda tile, r0: (tile == thr) & ((r0 + lax.broadcasted_iota(I32, (KEY_TILE, 1), 0)) < cand))
                return jnp.where(cnt <= need, cand, r)
            r = lax.fori_loop(0, 15, rbit, jnp.zeros((1, LANES), I32))
            return jnp.where(tied, r, big)

        return lax.cond(jnp.max(tied.astype(I32)) > 0, tie_search, lambda: big)

    row_lim = lax.cond(jnp.max(jnp.abs(cnt_thr - n_sel)) > 0, tie_path, lambda: big)
    eq_ok = thr > INT_MIN

    def bias_block(j, carry):
        r0 = pl.multiple_of(j * KEY_TILE, KEY_TILE)
        tile = key_ref[pl.ds(r0, KEY_TILE), :]
        row = r0 + lax.broadcasted_iota(I32, (KEY_TILE, 1), 0)
        sel = (tile > thr) | ((tile == thr) & (row < row_lim) & eq_ok)
        bias_ref[pl.ds(r0, KEY_TILE), :] = jnp.where(sel, 0.0, NEG)
        return carry

    lax.fori_loop(0, n_tiles, bias_block, 0)

    q_all = heads_on_lanes(q_ref[...].astype(F32).T.astype(BF16), N_HEADS_A, HEAD_DIM)
    m_ref[...] = jnp.full(m_ref.shape, NEG, F32)
    l_ref[...] = jnp.zeros(l_ref.shape, F32)
    acc_ref[...] = jnp.zeros(acc_ref.shape, F32)
    tiles_per_step = ATT_ROWS // KEY_TILE
    ones_rows = jnp.ones((16, ATT_ROWS), BF16)
    pair_w = 2 * LANES

    n_att = n_score * (SCORE_ROWS // ATT_ROWS)

    def logits(j, slot):
        r0 = pl.multiple_of(j * ATT_ROWS, ATT_ROWS)
        s_ref[slot] = _mm(kbf_ref[pl.ds(r0, ATT_ROWS), :], q_all)

    def att_block(j, slot):
        logits(jnp.minimum(j + 1, n_att - 1), 1 - slot)
        r0 = pl.multiple_of(j * ATT_ROWS, ATT_ROWS)
        vtb = jnp.concatenate([vt_ref[j * tiles_per_step + u] for u in range(tiles_per_step)], axis=1)
        vtb = jnp.concatenate([vtb, ones_rows], axis=0)
        bias = bias_ref[pl.ds(r0, ATT_ROWS), :]
        s = s_ref[slot] + jnp.concatenate([bias] * N_HEADS_A, axis=1)
        m_old = m_ref[...]
        m_new = jnp.maximum(m_old, jnp.max(s, axis=0, keepdims=True))
        alpha = jnp.exp(m_old - m_new)
        p = jnp.exp(s - m_new).astype(BF16)
        m_ref[...] = m_new
        for pr in range(N_HEADS_A // 2):
            cols = slice(pr * pair_w, (pr + 1) * pair_w)
            pv = _mm(vtb, p[:, cols])
            l_ref[:, cols] = alpha[:, cols] * l_ref[:, cols] + pv[HEAD_DIM:HEAD_DIM + 1, :]
            acc_ref[:, cols] = alpha[:, cols] * acc_ref[:, cols] + pv[:HEAD_DIM, :]

    def att_two(jj, carry):
        att_block(2 * jj, 0)
        att_block(2 * jj + 1, 1)
        return carry

    logits(0, 0)
    lax.fori_loop(0, n_att // 2, att_two, 0)
    out_t = acc_ref[...] * (1.0 / l_ref[...])
    o_ref[...] = jnp.concatenate([out_t[:, h * LANES:(h + 1) * LANES].T for h in range(N_HEADS_A)], axis=1).astype(BF16)


def _dsa(q, qi, kiwi, kbf, kibf, vt, *, n_batch, n_qblk, causal, n_sel, n_keys):
    sx = kbf.shape[1]
    qmap = lambda b, i: (b * n_qblk + i, 0)
    body = functools.partial(_dsa_body, causal=causal, n_sel=n_sel, n_keys=n_keys)
    return pl.pallas_call(
        body, grid=(n_batch, n_qblk),
        in_specs=[pl.BlockSpec((LANES, 1024), qmap), pl.BlockSpec((LANES, 512), qmap), pl.BlockSpec((LANES, LANES), qmap),
                  pl.BlockSpec((None, sx, HEAD_DIM), lambda b, i: (b, 0, 0)),
                  pl.BlockSpec((None, sx, HEAD_DIM_I), lambda b, i: (b, 0, 0)),
                  pl.BlockSpec((None, sx // KEY_TILE, HEAD_DIM, KEY_TILE), lambda b, i: (b, 0, 0, 0))],
        out_specs=pl.BlockSpec((LANES, 1024), qmap),
        out_shape=jax.ShapeDtypeStruct((n_batch * n_qblk * LANES, 1024), BF16),
        scratch_shapes=[pltpu.VMEM((sx, LANES), I32), pltpu.VMEM((sx, LANES), F32),
                        pltpu.VMEM((1, N_HEADS_A * LANES), F32), pltpu.VMEM((1, N_HEADS_A * LANES), F32),
                        pltpu.VMEM((HEAD_DIM, N_HEADS_A * LANES), F32),
                        pltpu.VMEM((2, ATT_ROWS, N_HEADS_A * LANES), F32)],
        compiler_params=pltpu.CompilerParams(dimension_semantics=("arbitrary", "arbitrary"), vmem_limit_bytes=VMEM_LIMIT),
        name="dsa",
    )(q, qi, kiwi, kbf, kibf, vt)


def _rwkv_body(rw_ref, shift0_ref, h0_ref, mu_ref, w0_ref, w2_ref, a0_ref, a2_ref, g2_ref,
               kk_ref, ka_ref, rk_ref, gng_ref, gnb_ref,
               out_ref, hfin_ref,
               prev_ref, st_ref, *, chunk):
    c = pl.program_id(1)
    n_c = pl.num_programs(1)
    C = chunk

    @pl.when(c == 0)
    def _():
        prev_ref[...] = shift0_ref[...]
        st_ref[...] = h0_ref[...]

    x = rw_ref[...]
    rowi = lax.broadcasted_iota(I32, (C, 1), 0)
    prev = jnp.where(rowi == 0, prev_ref[...], pltpu.roll(x, 1, 0))
    prev_ref[...] = x[C - 1:C, :]
    xs = x + (prev - x) * mu_ref[...]
    r = xs[:, 0:1024]
    k = xs[:, 1024:2048]
    v = xs[:, 2048:3072]
    lw = xs[:, 3072:3072 + LORA_W]
    la = xs[:, 3072 + LORA_W:3072 + LORA_W + LORA_A]
    lg = xs[:, 3072 + LORA_W + LORA_A:SHIFT_COLS]
    wx = w0_ref[...] + _dot_hi(jnp.tanh(lw), w2_ref[...])
    w_log = -jax.nn.softplus(-wx) - 0.5
    ld = -jnp.exp(w_log)
    a = jax.nn.sigmoid(a0_ref[...] + _dot_hi(la, a2_ref[...]))
    g = _dot_hi(jax.nn.sigmoid(lg), g2_ref[...])
    tri = (lax.broadcasted_iota(I32, (C, C), 0) >= lax.broadcasted_iota(I32, (C, C), 1)).astype(BF16)
    cl = _cumsum_rows(ld, tri)

    def pairs(t):
        return jnp.stack([t[:, p * LANES:(p + 1) * LANES] for p in range(RW_PAIRS)], axis=0)

    rp, kp, vp, ap, ldp, clp, gp = pairs(r), pairs(k), pairs(v), pairs(a), pairs(ld), pairs(cl), pairs(g)
    kkw, kaw, rkw, gng, gnb = kk_ref[...], ka_ref[...], rk_ref[...], gng_ref[...], gnb_ref[...]

    lane = lax.broadcasted_iota(I32, (1, 1, LANES), 2)
    lane_h0 = lane < RW_HEAD
    r2 = lax.broadcasted_iota(I32, (2 * C, 2 * C), 0)
    c2 = lax.broadcasted_iota(I32, (2 * C, 2 * C), 1)
    same_head = (r2 < C) == (c2 < C)
    strict = same_head & ((r2 % C) > (c2 % C))
    incl = same_head & ((r2 % C) >= (c2 % C))
    rl = lax.broadcasted_iota(I32, (LANES, LANES), 0)
    cl_ = lax.broadcasted_iota(I32, (LANES, LANES), 1)
    blockdiag = (rl < RW_HEAD) == (cl_ < RW_HEAD)
    eye = rl == cl_
    ones_bd = blockdiag.astype(BF16)

    def stack(t):
        return jnp.concatenate([jnp.where(lane_h0, t, 0.0), jnp.where(lane_h0, 0.0, t)], axis=1)

    def head_sum(t):
        return _dot_exact_rhs(t.reshape(RW_PAIRS * C, LANES), ones_bd).reshape(RW_PAIRS, C, LANES)

    bf = lambda t: t.astype(BF16)
    kk = kp * kkw
    kk = kk / jnp.maximum(jnp.sqrt(head_sum(kk * kk)), 1e-12)
    k2 = kp * (1.0 + (ap - 1.0) * kaw)
    e_in = jnp.exp(clp)
    e_ex = jnp.exp(clp - ldp)
    e_neg = jnp.exp(-clp)
    e_c = e_in[:, C - 1:C, :]
    at = -kk * e_ex
    rt = rp * e_in
    bt = kk * ap * e_neg
    kt = k2 * e_neg
    hbd = st_ref[...]

    a_s, r_s, b_s, k_s, v_s = stack(bf(at)), stack(bf(rt)), stack(bf(bt)), stack(bf(kt)), stack(bf(vp))
    ar = jnp.concatenate([a_s, r_s], axis=1)
    arh = _mm(ar, bf(hbd), _BNN)
    if (2 * C) % LANES == 0:
        m = _mm(ar, jnp.concatenate([b_s, k_s], axis=1), _BNT)
        m_ab, m_ak = m[:, :2 * C, :2 * C], m[:, :2 * C, 2 * C:]
        m_rb, m_rk = m[:, 2 * C:, :2 * C], m[:, 2 * C:, 2 * C:]
    else:
        m_ab, m_ak = _mm(a_s, b_s, _BNT), _mm(a_s, k_s, _BNT)
        m_rb, m_rk = _mm(r_s, b_s, _BNT), _mm(r_s, k_s, _BNT)
    xpow = bf(jnp.where(strict, m_ab, 0.0))
    m_ak = bf(jnp.where(strict, m_ak, 0.0))
    m_rb = bf(jnp.where(incl, m_rb, 0.0))
    m_rk = bf(jnp.where(incl, m_rk, 0.0))
    u_s = arh[:, :2 * C] + _mm(m_ak, v_s, _BNN)
    n = 1
    while n < C:
        u_s = u_s + _mm(xpow, bf(u_s), _BNN)
        n *= 2
        if n < C:
            xpow = bf(_mm(xpow, xpow, _BNN))
    if (2 * C) % LANES == 0:
        y_s = arh[:, 2 * C:] + _mm(jnp.concatenate([m_rb, m_rk], axis=2), jnp.concatenate([bf(u_s), v_s], axis=1), _BNN)
    else:
        y_s = arh[:, 2 * C:] + (_mm(m_rb, bf(u_s), _BNN) + _mm(m_rk, v_s, _BNN))
    y = y_s[:, :C] + y_s[:, C:]
    u = u_s[:, :C] + u_s[:, C:]
    dmat = jnp.where(eye, jnp.broadcast_to(e_c, (RW_PAIRS, LANES, LANES)), 0.0)
    lhs = jnp.concatenate([bt * e_c, kt * e_c, dmat], axis=1)
    rhs = jnp.concatenate([u, vp, hbd], axis=1)
    st_ref[...] = jnp.where(blockdiag, _dot_hi(lhs, rhs, _BTN), 0.0)

    mu_y = head_sum(y) * (1.0 / RW_HEAD)
    dy = y - mu_y
    var = head_sum(dy * dy) * (1.0 / RW_HEAD)
    yn = dy * lax.rsqrt(var + RW_GN_EPS) * gng + gnb
    bonus = head_sum(rp * k2 * rkw) * vp
    yo = ((yn + bonus) * gp).astype(BF16)
    out_ref[...] = jnp.concatenate([yo[p] for p in range(RW_PAIRS)], axis=1)

    @pl.when(c == n_c - 1)
    def _():
        hfin_ref[...] = st_ref[...]


def _cumsum_rows(x, tri_bf16):
    xh, xm, xl = _split3(x)
    return _mm(tri_bf16, xh) + (_mm(tri_bf16, xm) + _mm(tri_bf16, xl))


def _rwkv(rw, shift0, h0, w, *, n_batch, n_chunks, chunk, row_blocks_per_batch):
    body = functools.partial(_rwkv_body, chunk=chunk)
    pp = lambda: _const_spec((RW_PAIRS, 1, LANES))
    return pl.pallas_call(
        body, grid=(n_batch, n_chunks),
        in_specs=[pl.BlockSpec((chunk, SHIFT_COLS), lambda b, c: (b * row_blocks_per_batch + c, 0)),
                  pl.BlockSpec((None, 1, SHIFT_COLS), lambda b, c: (b, 0, 0)),
                  pl.BlockSpec((None, RW_PAIRS, LANES, LANES), lambda b, c: (b, 0, 0, 0)),
                  _const_spec((1, SHIFT_COLS)), _const_spec((1, D_MODEL)), _const_spec((LORA_W, D_MODEL)),
                  _const_spec((1, D_MODEL)), _const_spec((LORA_A, D_MODEL)), _const_spec((LORA_G, D_MODEL)),
                  pp(), pp(), pp(), pp(), pp()],
        out_specs=[pl.BlockSpec((chunk, D_MODEL), lambda b, c: (b * n_chunks + c, 0)),
                   pl.BlockSpec((None, RW_PAIRS, LANES, LANES), lambda b, c: (b, 0, 0, 0))],
        out_shape=[jax.ShapeDtypeStruct((n_batch * n_chunks * chunk, D_MODEL), BF16),
                   jax.ShapeDtypeStruct((n_batch, RW_PAIRS, LANES, LANES), F32)],
        scratch_shapes=[pltpu.VMEM((1, SHIFT_COLS), F32), pltpu.VMEM((RW_PAIRS, LANES, LANES), F32)],
        compiler_params=pltpu.CompilerParams(dimension_semantics=("arbitrary", "arbitrary"), vmem_limit_bytes=VMEM_LIMIT),
        name="rwkv",
    )(rw, shift0, h0, w["rw_mu"], w["rw_w0"], w["rw_w2"], w["rw_a0"], w["rw_a2"], w["rw_g2"],
      w["rw_k_k"], w["rw_k_a"], w["rw_r_k"], w["rw_gn_g"], w["rw_gn_b"])


def _merge_body(x_ref, oa_ref, orw_ref, g0_ref, b0_ref, wg_ref, woa_ref, wor_ref, wout_ref, g1_ref, b1_ref, x1_ref):
    h = _layernorm(x_ref[...], g0_ref[...], b0_ref[...])
    gates = jax.nn.sigmoid(_mm(h.astype(BF16), wg_ref[...]))
    o_attn = _mm(oa_ref[...], woa_ref[...])
    o_rw = _mm(orw_ref[...], wor_ref[...])
    mixed = _mm((gates[:, :D_MODEL] * o_attn + gates[:, D_MODEL:] * o_rw).astype(BF16), wout_ref[...])
    x1_ref[...] = _layernorm(ALPHA * h + mixed, g1_ref[...], b1_ref[...])


def _merge(x, oa, orw, w, tm):
    rows = x.shape[0]
    spec = lambda: pl.BlockSpec((tm, D_MODEL), lambda i: (i, 0))
    vec = lambda: _const_spec((1, D_MODEL))
    sq = lambda: _const_spec((D_MODEL, D_MODEL))
    return pl.pallas_call(
        _merge_body, grid=(rows // tm,),
        in_specs=[spec(), spec(), spec(), vec(), vec(), _const_spec((D_MODEL, 2 * D_MODEL)), sq(), sq(), sq(), vec(), vec()],
        out_specs=spec(), out_shape=jax.ShapeDtypeStruct((rows, D_MODEL), F32),
        compiler_params=pltpu.CompilerParams(dimension_semantics=("arbitrary",), vmem_limit_bytes=VMEM_LIMIT),
        name="merge",
    )(x, oa, orw, w["ln0_g"], w["ln0_b"], w["wg"], w["w_o_attn"], w["w_o_rwkv"], w["w_out"], w["ln1_g"], w["ln1_b"])


def _ffn_body(x1_ref, wgate_ref, wup_ref, wdown_ref, g2_ref, b2_ref, y_ref):
    x1 = x1_ref[...]
    xb = x1.astype(BF16)
    hidden = jax.nn.silu(_mm(xb, wgate_ref[...])) * _mm(xb, wup_ref[...])
    ffn = _mm(hidden.astype(BF16), wdown_ref[...])
    y_ref[...] = _layernorm(ALPHA * x1 + ffn, g2_ref[...], b2_ref[...])


def _ffn(x1, w, tm):
    rows = x1.shape[0]
    spec = lambda: pl.BlockSpec((tm, D_MODEL), lambda i: (i, 0))
    return pl.pallas_call(
        _ffn_body, grid=(rows // tm,),
        in_specs=[spec(), _const_spec((D_MODEL, D_FF)), _const_spec((D_MODEL, D_FF)), _const_spec((D_FF, D_MODEL)),
                  _const_spec((1, D_MODEL)), _const_spec((1, D_MODEL))],
        out_specs=spec(), out_shape=jax.ShapeDtypeStruct((rows, D_MODEL), F32),
        compiler_params=pltpu.CompilerParams(dimension_semantics=("arbitrary",), vmem_limit_bytes=VMEM_LIMIT),
        name="ffn",
    )(x1, w["ffn_w_gate"], w["ffn_w_up"], w["ffn_w_down"], w["ln2_g"], w["ln2_b"])


def _rope_table(pos):
    pos = pos.astype(F32)[:, None]

    def one(rot, period):
        half = rot // 2
        inv = ROPE_THETA ** (-jnp.arange(0, rot, 2, dtype=F32) / rot)
        ang = pos * inv[None]
        cos, sin = jnp.cos(ang), jnp.sin(ang)
        t = pos.shape[0]
        ones = jnp.ones((t, period - rot), F32)
        zeros = jnp.zeros((t, period - rot), F32)
        c = jnp.concatenate([cos, cos, ones], axis=1)
        s = jnp.concatenate([-sin, sin, zeros], axis=1)
        reps = LANES // period
        return jnp.tile(c, (1, reps)), jnp.tile(s, (1, reps))

    ca, sa = one(ROPE_DIM_A, HEAD_DIM)
    ci, si = one(ROPE_DIM_I, HEAD_DIM_I)
    return jnp.concatenate([ca, sa, ci, si], axis=1)


def _pairs(vec):
    return vec.reshape(RW_PAIRS, 1, LANES).astype(F32)


def _state_to_pairs(s):
    b = s.shape[0]
    ht = jnp.swapaxes(s, -1, -2).reshape(b, RW_PAIRS, 2, RW_HEAD, RW_HEAD)
    z = jnp.zeros((b, RW_PAIRS, RW_HEAD, RW_HEAD), s.dtype)
    top = jnp.concatenate([ht[:, :, 0], z], axis=-1)
    bot = jnp.concatenate([z, ht[:, :, 1]], axis=-1)
    return jnp.concatenate([top, bot], axis=-2)


def _pairs_to_state(h):
    b = h.shape[0]
    h0 = h[:, :, :RW_HEAD, :RW_HEAD]
    h1 = h[:, :, RW_HEAD:, RW_HEAD:]
    ht = jnp.stack([h0, h1], axis=2).reshape(b, RW_HEADS, RW_HEAD, RW_HEAD)
    return jnp.swapaxes(ht, -1, -2)


def _pad_rows(a, rows, axis=1):
    pad = [(0, 0)] * a.ndim
    pad[axis] = (0, rows - a.shape[axis])
    return jnp.pad(a, pad)


def kernel(x_prompt, x_sample, cache_k, cache_v, cache_idx_k, state_wkv, state_shift, meta_tokens, ln0_g, ln0_b, w_in, idx_k_ln_g, idx_k_ln_b, rw_mu, rw_w0, rw_w2, rw_a0, rw_a2, rw_g2, rw_k_k, rw_k_a, rw_r_k, rw_gn_g, rw_gn_b, w_o_attn, w_o_rwkv, w_out, ln1_g, ln1_b, ffn_w_gate, ffn_w_up, ffn_w_down, ln2_g, ln2_b):
    B, S, _ = x_prompt.shape
    Bs, Ts, _ = x_sample.shape
    P = cache_k.shape[2] - N_META
    depth = w_in.shape[0]
    assert depth == 1 and S % (2 * CHUNK) == 0 and P % KEY_TILE == 0 and Ts == CHUNK
    l = 0
    row = lambda v: v.reshape(1, -1).astype(F32)
    win = w_in[l]
    wi_pack = jnp.concatenate([win[:, _QI0:_G0], jnp.zeros((D_MODEL, 640 - (_G0 - _QI0)), F32)], axis=1)
    lane_pad = lambda v: jnp.concatenate([v, jnp.zeros((LANES - HEAD_DIM_I,), F32)]).reshape(1, LANES)
    w = {
        "ln0_g": row(ln0_g), "ln0_b": row(ln0_b),
        "wa": win[:, _Q0:_QI0].astype(BF16), "wi": wi_pack.astype(BF16), "wr": win[:, _RW0:].astype(BF16),
        "wg": win[:, _G0:_RW0].astype(BF16),
        "ikg": lane_pad(idx_k_ln_g[l]), "ikb": lane_pad(idx_k_ln_b[l]),
        "rw_mu": row(rw_mu[l]), "rw_w0": row(rw_w0[l]), "rw_w2": rw_w2[l], "rw_a0": row(rw_a0[l]),
        "rw_a2": rw_a2[l], "rw_g2": rw_g2[l],
        "rw_k_k": _pairs(rw_k_k[l]), "rw_k_a": _pairs(rw_k_a[l]), "rw_r_k": _pairs(rw_r_k[l]),
        "rw_gn_g": _pairs(rw_gn_g[l]), "rw_gn_b": _pairs(rw_gn_b[l]),
        "w_o_attn": w_o_attn[l].astype(BF16), "w_o_rwkv": w_o_rwkv[l].astype(BF16), "w_out": w_out[l].astype(BF16),
        "ln1_g": row(ln1_g[l]), "ln1_b": row(ln1_b[l]),
        "ffn_w_gate": ffn_w_gate[l].astype(BF16), "ffn_w_up": ffn_w_up[l].astype(BF16),
        "ffn_w_down": ffn_w_down[l].astype(BF16), "ln2_g": row(ln2_g[l]), "ln2_b": row(ln2_b[l]),
    }

    tm = 256
    x_meta = _pad_rows(meta_tokens.astype(F32), KEY_TILE, axis=0)
    m_q, m_k, m_v, m_kbf, m_vt, m_qi, m_kiwi, m_kibf, m_rw = _proj(
        x_meta, _rope_table(jnp.arange(KEY_TILE)), 1, KEY_TILE, w)
    xf = x_prompt.reshape(B * S, D_MODEL)
    f_q, f_k, f_v, f_kbf, f_vt, f_qi, f_kiwi, f_kibf, f_rw = _proj(
        xf, _rope_table(N_META + jnp.arange(S)), S // tm, tm, w)
    xs_pad = _pad_rows(x_sample, LANES, axis=1).reshape(Bs * LANES, D_MODEL)
    s_q, s_k, s_v, s_kbf, s_vt, s_qi, s_kiwi, s_kibf, s_rw = _proj(
        xs_pad, _rope_table(N_META + P + jnp.arange(LANES)), 1, LANES, w)

    def key_rows(meta_tile, frames, total):
        b = frames.shape[0]
        mt = jnp.broadcast_to(meta_tile[None], (b,) + meta_tile.shape)
        return _pad_rows(jnp.concatenate([mt, frames], axis=1), total, axis=1)

    sx_p = -(-(KEY_TILE + S) // SCORE_ROWS) * SCORE_ROWS
    o_attn_p = _dsa(
        f_q, f_qi, f_kiwi,
        key_rows(m_kbf, f_kbf.reshape(B, S, HEAD_DIM), sx_p),
        key_rows(m_kibf, f_kibf.reshape(B, S, HEAD_DIM_I), sx_p),
        key_rows(m_vt, f_vt.reshape(B, S // KEY_TILE, HEAD_DIM, KEY_TILE), sx_p // KEY_TILE),
        n_batch=B, n_qblk=S // LANES, causal=True, n_sel=min(TOPK_MAX, S // 4), n_keys=S)

    n_keys_s = P + Ts
    sx_s = -(-(KEY_TILE + P + LANES) // SCORE_ROWS) * SCORE_ROWS
    meta_rows = lambda c: _pad_rows(c[l][:, :N_META], KEY_TILE, axis=1)
    past_vt = jnp.swapaxes(cache_v[l][:, N_META:].reshape(Bs, P // KEY_TILE, KEY_TILE, HEAD_DIM), -1, -2)
    meta_vt = jnp.swapaxes(meta_rows(cache_v), -1, -2)[:, None]
    kbf_s = _pad_rows(jnp.concatenate([meta_rows(cache_k).astype(BF16), cache_k[l][:, N_META:].astype(BF16),
                                       s_kbf.reshape(Bs, LANES, HEAD_DIM)], axis=1), sx_s)
    kibf_s = _pad_rows(jnp.concatenate([meta_rows(cache_idx_k).astype(BF16), cache_idx_k[l][:, N_META:].astype(BF16),
                                        s_kibf.reshape(Bs, LANES, HEAD_DIM_I)], axis=1), sx_s)
    vt_s = _pad_rows(jnp.concatenate([meta_vt.astype(BF16), past_vt.astype(BF16),
                                      s_vt.reshape(Bs, 1, HEAD_DIM, KEY_TILE)], axis=1), sx_s // KEY_TILE)
    o_attn_s = _dsa(s_q, s_qi, s_kiwi, kbf_s, kibf_s, vt_s,
                    n_batch=Bs, n_qblk=1, causal=False, n_sel=min(TOPK_MAX, n_keys_s // 4), n_keys=n_keys_s)
    o_attn_s = o_attn_s.reshape(Bs, LANES, D_MODEL)[:, :Ts].reshape(Bs * Ts, D_MODEL)

    zero_shift = jnp.zeros((1, 1, SHIFT_COLS), F32)
    zero_state = jnp.zeros((1, RW_PAIRS, LANES, LANES), F32)
    _, h_meta = _rwkv(m_rw, zero_shift, zero_state, w, n_batch=1, n_chunks=1, chunk=N_META, row_blocks_per_batch=0)
    shift_meta = m_rw[N_META - 1:N_META].reshape(1, 1, SHIFT_COLS)
    o_rw_p, h_p = _rwkv(f_rw, jnp.broadcast_to(shift_meta, (B, 1, SHIFT_COLS)),
                        jnp.broadcast_to(h_meta, (B, RW_PAIRS, LANES, LANES)), w,
                        n_batch=B, n_chunks=S // CHUNK, chunk=CHUNK, row_blocks_per_batch=S // CHUNK)
    o_rw_s, h_s = _rwkv(s_rw, state_shift[l].reshape(Bs, 1, SHIFT_COLS), _state_to_pairs(state_wkv[l]), w,
                        n_batch=Bs, n_chunks=1, chunk=CHUNK, row_blocks_per_batch=LANES // CHUNK)

    y_p = _ffn(_merge(xf, o_attn_p, o_rw_p, w, tm), w, tm)
    xs_flat = x_sample.reshape(Bs * Ts, D_MODEL)
    tm_s = min(tm, Bs * Ts)
    y_s = _ffn(_merge(xs_flat, o_attn_s, o_rw_s, w, tm_s), w, tm_s)

    def with_meta(meta_rows_, frames, width):
        mt = jnp.broadcast_to(meta_rows_[None, :N_META, :width], (B, N_META, width))
        return jnp.concatenate([mt, frames.reshape(B, S, -1)[:, :, :width]], axis=1)[None]

    take = lambda a, width: a.reshape(Bs, LANES, -1)[:, :Ts, :width][None]
    return (
        y_p.reshape(B, S, D_MODEL),
        y_s.reshape(Bs, Ts, D_MODEL),
        with_meta(m_k, f_k, HEAD_DIM),
        with_meta(m_v, f_v, HEAD_DIM),
        with_meta(m_kiwi, f_kiwi, HEAD_DIM_I),
        _pairs_to_state(h_p)[None],
        f_rw.reshape(B, S, SHIFT_COLS)[:, -1][None],
        take(s_k, HEAD_DIM),
        take(s_v, HEAD_DIM),
        take(s_kiwi, HEAD_DIM_I),
        _pairs_to_state(h_s)[None],
        s_rw.reshape(Bs, LANES, SHIFT_COLS)[:, Ts - 1][None],
    )
```

```python
import functools

import numpy as np
import jax
import jax.numpy as jnp
from jax import lax
from jax.experimental import pallas as pl
from jax.experimental.pallas import tpu as pltpu

F32 = jnp.float32
BF16 = jnp.bfloat16
I32 = jnp.int32

D_MODEL = 1024
N_META = 16
CHUNK = 64
N_HEADS_A = 8
HEAD_DIM = 128
ROPE_DIM_A = HEAD_DIM // 4
N_HEADS_I = 8
HEAD_DIM_I = 64
ROPE_DIM_I = HEAD_DIM_I // 4
TOPK_MAX = 256
ROPE_THETA = 500000.0
RW_HEAD = 64
RW_HEADS = 16
RW_PAIRS = RW_HEADS // 2
LORA_W = 64
LORA_A = 64
LORA_G = 128
RW_GN_EPS = 64e-5
D_FF = 2816
LN_EPS = 1e-5
ALPHA = 2.0 ** 0.25
NEG = -1e30
LOG2_E = 1.4426950408889634
SHIFT_COLS = 3 * D_MODEL + LORA_W + LORA_A + LORA_G
_Q0, _K0, _V0, _QI0, _KI0, _WI0, _G0, _RW0 = 0, 1024, 1152, 1280, 1792, 1856, 1864, 3912
_IN_COLS = _RW0 + SHIFT_COLS

LANES = 128
KEY_TILE = 128
SCORE_ROWS = 512
ATT_ROWS = 256
INT_MIN = -2 ** 31
VMEM_LIMIT = 56 * 1024 * 1024

_NN = (((1,), (0,)), ((), ()))
_NT = (((1,), (1,)), ((), ()))
_TN = (((0,), (0,)), ((), ()))
_BNN = (((2,), (1,)), ((0,), (0,)))
_BNT = (((2,), (2,)), ((0,), (0,)))
_BTN = (((1,), (1,)), ((0,), (0,)))


def _mm(a, b, dn=_NN):
    return lax.dot_general(a, b, dn, preferred_element_type=F32)


def _split3(a):
    hi = a.astype(BF16)
    r1 = a - hi.astype(F32)
    mid = r1.astype(BF16)
    lo = (r1 - mid.astype(F32)).astype(BF16)
    return hi, mid, lo


def _dot_hi(a, b, dn=_NN):
    ah, am, _ = _split3(a)
    bh, bm, _ = _split3(b)
    return _mm(ah, bh, dn) + (_mm(ah, bm, dn) + _mm(am, bh, dn))


def _dot_exact_rhs(a, b_bf16, dn=_NN):
    ah = a.astype(BF16)
    am = (a - ah.astype(F32)).astype(BF16)
    return _mm(ah, b_bf16, dn) + _mm(am, b_bf16, dn)


def _layernorm(x, g, b):
    mu = jnp.mean(x, axis=-1, keepdims=True)
    d = x - mu
    var = jnp.mean(d * d, axis=-1, keepdims=True)
    return d * lax.rsqrt(var + LN_EPS) * g + b


def _const_spec(shape):
    nd = len(shape)
    return pl.BlockSpec(shape, lambda *_: (0,) * nd, pipeline_mode=pl.Buffered(1))


def _rope(x, cos, sin, period, half):
    width = x.shape[1]
    reps = width // LANES
    cc = jnp.concatenate([cos] * reps, axis=1) if reps > 1 else cos
    ss = jnp.concatenate([sin] * reps, axis=1) if reps > 1 else sin
    lane = lax.broadcasted_iota(I32, (1, width), 1) % period
    fwd = pltpu.roll(x, width - half, 1)
    bwd = pltpu.roll(x, half, 1)
    return x * cc + jnp.where(lane < half, fwd, bwd) * ss


def _proj_body(x_ref, tab_ref, g0_ref, b0_ref, wa_ref, wi_ref, wr_ref, ikg_ref, ikb_ref,
               q_ref, k_ref, v_ref, kbf_ref, vt_ref, qi_ref, kiwi_ref, kibf_ref, rw_ref):
    tm = x_ref.shape[0]
    hb = _layernorm(x_ref[...], g0_ref[...], b0_ref[...]).astype(BF16)
    tab = tab_ref[...]
    cos_a, sin_a = tab[:, 0:128], tab[:, 128:256]
    cos_i, sin_i = tab[:, 256:384], tab[:, 384:512]

    pa = _mm(hb, wa_ref[...])
    q = _rope(pa[:, :1024], cos_a, sin_a, HEAD_DIM, ROPE_DIM_A // 2)
    q_ref[...] = (q * (HEAD_DIM ** -0.5 * LOG2_E)).astype(BF16)
    k = _rope(pa[:, 1024:1152], cos_a, sin_a, HEAD_DIM, ROPE_DIM_A // 2)
    k_ref[...] = k
    kbf_ref[...] = k.astype(BF16)
    v = pa[:, 1152:1280]
    v_ref[...] = v
    for j in range(tm // KEY_TILE):
        vt_ref[j] = v[j * KEY_TILE:(j + 1) * KEY_TILE, :].T.astype(BF16)

    pi = _mm(hb, wi_ref[...])
    qi_ref[...] = _rope(pi[:, :512], cos_i, sin_i, HEAD_DIM_I, ROPE_DIM_I // 2).astype(BF16)
    kw = pi[:, 512:640]
    lane = lax.broadcasted_iota(I32, (1, LANES), 1)
    is_ki = lane < HEAD_DIM_I
    mu = jnp.sum(jnp.where(is_ki, kw, 0.0), axis=-1, keepdims=True) * (1.0 / HEAD_DIM_I)
    d = jnp.where(is_ki, kw - mu, 0.0)
    var = jnp.sum(d * d, axis=-1, keepdims=True) * (1.0 / HEAD_DIM_I)
    ki = d * lax.rsqrt(var + LN_EPS) * ikg_ref[...] + ikb_ref[...]
    ki = _rope(ki, cos_i, sin_i, HEAD_DIM_I, ROPE_DIM_I // 2)
    wi_scale = (N_HEADS_I ** -0.5) * (HEAD_DIM_I ** -0.5)
    kiwi_ref[...] = jnp.where(is_ki, ki, jnp.where(lane < HEAD_DIM_I + N_HEADS_I, kw * wi_scale, 0.0))
    kibf_ref[...] = ki[:, :HEAD_DIM_I].astype(BF16)

    rw_ref[...] = _mm(hb, wr_ref[...])


def _proj(x, tab, tab_blocks, tm, w):
    rows = x.shape[0]
    n = rows // tm
    row_spec = lambda width: pl.BlockSpec((tm, width), lambda i: (i, 0))
    out_shape = [
        jax.ShapeDtypeStruct((rows, 1024), BF16),
        jax.ShapeDtypeStruct((rows, HEAD_DIM), F32),
        jax.ShapeDtypeStruct((rows, HEAD_DIM), F32),
        jax.ShapeDtypeStruct((rows, HEAD_DIM), BF16),
        jax.ShapeDtypeStruct((rows // KEY_TILE, HEAD_DIM, KEY_TILE), BF16),
        jax.ShapeDtypeStruct((rows, 512), BF16),
        jax.ShapeDtypeStruct((rows, LANES), F32),
        jax.ShapeDtypeStruct((rows, HEAD_DIM_I), BF16),
        jax.ShapeDtypeStruct((rows, SHIFT_COLS), F32),
    ]
    out_specs = [row_spec(1024), row_spec(HEAD_DIM), row_spec(HEAD_DIM), row_spec(HEAD_DIM),
                 pl.BlockSpec((tm // KEY_TILE, HEAD_DIM, KEY_TILE), lambda i: (i, 0, 0)),
                 row_spec(512), row_spec(LANES), row_spec(HEAD_DIM_I), row_spec(SHIFT_COLS)]
    in_specs = [row_spec(D_MODEL),
                pl.BlockSpec((tm, 512), lambda i: (i % tab_blocks, 0)),
                _const_spec((1, D_MODEL)), _const_spec((1, D_MODEL)),
                _const_spec((D_MODEL, 1280)), _const_spec((D_MODEL, 640)), _const_spec((D_MODEL, SHIFT_COLS)),
                _const_spec((1, LANES)), _const_spec((1, LANES))]
    return pl.pallas_call(
        _proj_body, grid=(n,), in_specs=in_specs, out_specs=out_specs, out_shape=out_shape,
        compiler_params=pltpu.CompilerParams(dimension_semantics=("arbitrary",), vmem_limit_bytes=VMEM_LIMIT),
        name="proj",
    )(x, tab, w["ln0_g"], w["ln0_b"], w["wa"], w["wi"], w["wr"], w["ikg"], w["ikb"])


def _dsa_body(q_ref, qi_ref, kiwi_ref, kbf_ref, kibf_ref, vt_ref, o_ref,
              sc_ref, bias_ref, m_ref, l_ref, acc_ref, s_ref, *, causal, n_sel, n_keys):
    i = pl.program_id(1)
    if causal:
        lim_lo = KEY_TILE + i * (2 * CHUNK) + CHUNK
        lim_hi = lim_lo + CHUNK
    else:
        lim_lo = KEY_TILE + n_keys
        lim_hi = lim_lo
    n_score = (lim_hi + SCORE_ROWS - 1) // SCORE_ROWS

    heads_on_lanes = lambda t, n, rows: jnp.concatenate([t[h * rows:(h + 1) * rows, :] for h in range(n)], axis=1)
    qi_all = heads_on_lanes(qi_ref[...].astype(F32).T.astype(BF16), N_HEADS_I, HEAD_DIM_I)
    w_all = heads_on_lanes(kiwi_ref[...].T[HEAD_DIM_I:HEAD_DIM_I + N_HEADS_I, :], N_HEADS_I, 1)
    lane = lax.broadcasted_iota(I32, (1, LANES), 1)

    def score_block(j, carry):
        r0 = pl.multiple_of(j * SCORE_ROWS, SCORE_ROWS)
        s = jnp.maximum(_mm(kibf_ref[pl.ds(r0, SCORE_ROWS), :], qi_all), 0.0) * w_all
        sc = s[:, :LANES]
        for h in range(1, N_HEADS_I):
            sc = sc + s[:, h * LANES:(h + 1) * LANES]
        row = r0 + lax.broadcasted_iota(I32, (SCORE_ROWS, 1), 0)
        adm = (row < N_META) | ((row >= KEY_TILE) & (row < lim_lo)) | ((row >= lim_lo) & (row < lim_hi) & (lane >= CHUNK))
        sc_ref[pl.ds(r0, SCORE_ROWS), :] = jnp.where(adm, sc, NEG)
        return carry

    lax.fori_loop(0, n_score, score_block, 0)
    tiles_per_blk = SCORE_ROWS // KEY_TILE
    n_tiles = n_score * tiles_per_blk

    def count(pred):
        def body(j, acc):
            for u in range(tiles_per_blk):
                r0 = pl.multiple_of(j * SCORE_ROWS + u * KEY_TILE, KEY_TILE)
                m = pred(sc_ref[pl.ds(r0, KEY_TILE), :], r0)
                acc = jnp.where(m.reshape(KEY_TILE // 8, 8, LANES), acc + 1, acc)
            return acc
        acc = lax.fori_loop(0, n_score, body, jnp.zeros((KEY_TILE // 8, 8, LANES), I32))
        return acc.sum(axis=0).sum(axis=0, keepdims=True)

    def as_score(k):
        return lax.bitcast_convert_type(k ^ ((k >> 31) & 0x7FFFFFFF), F32)

    def bit_step(b, st):
        t, cnt_t = st
        cand = t + lax.shift_left(jnp.int32(1), 31 - b)
        cand_f = as_score(cand)
        cnt = count(lambda tile, r0: tile >= cand_f)
        take = cnt >= n_sel
        return jnp.where(take, cand, t), jnp.where(take, cnt, cnt_t)

    all_rows = jnp.zeros((1, LANES), I32) + n_score * SCORE_ROWS
    thr_i, cnt_thr = lax.fori_loop(0, 32, bit_step, (jnp.full((1, LANES), INT_MIN, I32), all_rows))
    thr = as_score(thr_i)
    big = jnp.full((1, LANES), 2 ** 30, I32)

    def tie_path():
        cnt_gt = count(lambda tile, r0: tile > thr)
        need = n_sel - cnt_gt
        tied = cnt_thr - cnt_gt > need

        def tie_search():
            def rbit(b, r):
                cand = r + lax.shift_left(jnp.int32(1), 14 - b)
                cnt = count(lambda tile, r0: (tile == thr) & ((r0 + lax.broadcasted_iota(I32, (KEY_TILE, 1), 0)) < cand))
                return jnp.where(cnt <= need, cand, r)
            r = lax.fori_loop(0, 15, rbit, jnp.zeros((1, LANES), I32))
            return jnp.where(tied, r, big)

        return lax.cond(jnp.max(tied.astype(I32)) > 0, tie_search, lambda: big)

    row_lim = lax.cond(jnp.max(jnp.abs(cnt_thr - n_sel)) > 0, tie_path, lambda: big)

    def bias_block(j, carry):
        r0 = pl.multiple_of(j * KEY_TILE, KEY_TILE)
        tile = sc_ref[pl.ds(r0, KEY_TILE), :]
        row = r0 + lax.broadcasted_iota(I32, (KEY_TILE, 1), 0)
        sel = ((tile > thr) | ((tile == thr) & (row < row_lim))) & (tile > 0.5 * NEG)
        bias_ref[pl.ds(r0, KEY_TILE), :] = jnp.where(sel, 0.0, NEG)
        return carry

    lax.fori_loop(0, n_tiles, bias_block, 0)

    q_all = heads_on_lanes(q_ref[...].astype(F32).T.astype(BF16), N_HEADS_A, HEAD_DIM)
    m_ref[...] = jnp.full(m_ref.shape, NEG, F32)
    l_ref[...] = jnp.zeros(l_ref.shape, F32)
    acc_ref[...] = jnp.zeros(acc_ref.shape, F32)
    tiles_per_step = ATT_ROWS // KEY_TILE
    ones_rows = jnp.ones((16, ATT_ROWS), BF16)
    pair_w = 2 * LANES

    n_att = n_score * (SCORE_ROWS // ATT_ROWS)

    def logits(j, slot):
        r0 = pl.multiple_of(j * ATT_ROWS, ATT_ROWS)
        s_ref[slot] = _mm(kbf_ref[pl.ds(r0, ATT_ROWS), :], q_all)

    def att_block(j, slot):
        logits(jnp.minimum(j + 1, n_att - 1), 1 - slot)
        r0 = pl.multiple_of(j * ATT_ROWS, ATT_ROWS)
        vtb = jnp.concatenate([vt_ref[j * tiles_per_step + u] for u in range(tiles_per_step)], axis=1)
        vtb = jnp.concatenate([vtb, ones_rows], axis=0)
        bias = bias_ref[pl.ds(r0, ATT_ROWS), :]
        s = s_ref[slot] + jnp.concatenate([bias] * N_HEADS_A, axis=1)
        m_old = m_ref[...]
        m_new = jnp.maximum(m_old, jnp.max(s, axis=0, keepdims=True))
        alpha = jnp.exp2(m_old - m_new)
        p = jnp.exp2(s - m_new).astype(BF16)
        m_ref[...] = m_new
        for pr in range(N_HEADS_A // 2):
            cols = slice(pr * pair_w, (pr + 1) * pair_w)
            pv = _mm(vtb, p[:, cols])
            l_ref[:, cols] = alpha[:, cols] * l_ref[:, cols] + pv[HEAD_DIM:HEAD_DIM + 1, :]
            acc_ref[:, cols] = alpha[:, cols] * acc_ref[:, cols] + pv[:HEAD_DIM, :]

    def att_two(jj, carry):
        att_block(2 * jj, 0)
        att_block(2 * jj + 1, 1)
        return carry

    logits(0, 0)
    lax.fori_loop(0, n_att // 2, att_two, 0)
    out_t = acc_ref[...] * (1.0 / l_ref[...])
    o_ref[...] = jnp.concatenate([out_t[:, h * LANES:(h + 1) * LANES].T for h in range(N_HEADS_A)], axis=1).astype(BF16)


def _dsa(q, qi, kiwi, kbf, kibf, vt, *, n_batch, n_qblk, causal, n_sel, n_keys):
    sx = kbf.shape[1]
    qmap = lambda b, i: (b * n_qblk + i, 0)
    body = functools.partial(_dsa_body, causal=causal, n_sel=n_sel, n_keys=n_keys)
    return pl.pallas_call(
        body, grid=(n_batch, n_qblk),
        in_specs=[pl.BlockSpec((LANES, 1024), qmap), pl.BlockSpec((LANES, 512), qmap), pl.BlockSpec((LANES, LANES), qmap),
                  pl.BlockSpec((None, sx, HEAD_DIM), lambda b, i: (b, 0, 0)),
                  pl.BlockSpec((None, sx, HEAD_DIM_I), lambda b, i: (b, 0, 0)),
                  pl.BlockSpec((None, sx // KEY_TILE, HEAD_DIM, KEY_TILE), lambda b, i: (b, 0, 0, 0))],
        out_specs=pl.BlockSpec((LANES, 1024), qmap),
        out_shape=jax.ShapeDtypeStruct((n_batch * n_qblk * LANES, 1024), BF16),
        scratch_shapes=[pltpu.VMEM((sx, LANES), F32), pltpu.VMEM((sx, LANES), F32),
                        pltpu.VMEM((1, N_HEADS_A * LANES), F32), pltpu.VMEM((1, N_HEADS_A * LANES), F32),
                        pltpu.VMEM((HEAD_DIM, N_HEADS_A * LANES), F32),
                        pltpu.VMEM((2, ATT_ROWS, N_HEADS_A * LANES), F32)],
        compiler_params=pltpu.CompilerParams(dimension_semantics=("arbitrary", "arbitrary"), vmem_limit_bytes=VMEM_LIMIT),
        name="dsa",
    )(q, qi, kiwi, kbf, kibf, vt)


def _rwkv_body(rw_ref, shift0_ref, h0_ref, mu_ref, w0_ref, w2h_ref, w2m_ref, a0_ref, a2_ref, g2_ref,
               kk_ref, ka_ref, rk_ref, gng_ref, gnb_ref,
               out_ref, hfin_ref,
               prev_ref, st_ref, *, chunk):
    c = pl.program_id(1)
    n_c = pl.num_programs(1)
    C = chunk

    @pl.when(c == 0)
    def _():
        prev_ref[...] = shift0_ref[...]
        st_ref[...] = h0_ref[...]

    x = rw_ref[...]
    rowi = lax.broadcasted_iota(I32, (C, 1), 0)
    prev = jnp.where(rowi == 0, prev_ref[...], pltpu.roll(x, 1, 0))
    prev_ref[...] = x[C - 1:C, :]
    xs = x + (prev - x) * mu_ref[...]
    r = xs[:, 0:1024]
    k = xs[:, 1024:2048]
    v = xs[:, 2048:3072]
    lw = xs[:, 3072:3072 + LORA_W]
    la = xs[:, 3072 + LORA_W:3072 + LORA_W + LORA_A]
    lg = xs[:, 3072 + LORA_W + LORA_A:SHIFT_COLS]
    tw = jnp.tanh(lw)
    twh = tw.astype(BF16)
    twm = (tw - twh.astype(F32)).astype(BF16)
    wx = w0_ref[...] + (_mm(twh, w2h_ref[...]) + (_mm(twh, w2m_ref[...]) + _mm(twm, w2h_ref[...])))
    w_log = -jax.nn.softplus(-wx) - 0.5
    ld = -jnp.exp(w_log)
    a = jax.nn.sigmoid(a0_ref[...] + _mm(la.astype(BF16), a2_ref[...]))
    g = _mm(jax.nn.sigmoid(lg).astype(BF16), g2_ref[...])
    tri = (lax.broadcasted_iota(I32, (C, C), 0) >= lax.broadcasted_iota(I32, (C, C), 1)).astype(BF16)
    cl = _cumsum_rows(ld, tri)

    def pairs(t):
        return jnp.stack([t[:, p * LANES:(p + 1) * LANES] for p in range(RW_PAIRS)], axis=0)

    rp, kp, vp, ap, ldp, clp, gp = pairs(r), pairs(k), pairs(v), pairs(a), pairs(ld), pairs(cl), pairs(g)
    kkw, kaw, rkw, gng, gnb = kk_ref[...], ka_ref[...], rk_ref[...], gng_ref[...], gnb_ref[...]

    lane = lax.broadcasted_iota(I32, (1, 1, LANES), 2)
    lane_h0 = lane < RW_HEAD
    r2 = lax.broadcasted_iota(I32, (2 * C, 2 * C), 0)
    c2 = lax.broadcasted_iota(I32, (2 * C, 2 * C), 1)
    same_head = (r2 < C) == (c2 < C)
    strict = same_head & ((r2 % C) > (c2 % C))
    incl = same_head & ((r2 % C) >= (c2 % C))
    rl = lax.broadcasted_iota(I32, (LANES, LANES), 0)
    cl_ = lax.broadcasted_iota(I32, (LANES, LANES), 1)
    blockdiag = (rl < RW_HEAD) == (cl_ < RW_HEAD)
    eye = rl == cl_
    ones_bd = blockdiag.astype(BF16)

    def stack(t):
        return jnp.concatenate([jnp.where(lane_h0, t, 0.0), jnp.where(lane_h0, 0.0, t)], axis=1)

    def head_sum(t, split=False):
        t2 = t.reshape(RW_PAIRS * C, LANES)
        s = _dot_exact_rhs(t2, ones_bd) if split else _mm(t2.astype(BF16), ones_bd)
        return s.reshape(RW_PAIRS, C, LANES)

    bf = lambda t: t.astype(BF16)
    kk = kp * kkw
    kk = kk / jnp.maximum(jnp.sqrt(head_sum(kk * kk, split=True)), 1e-12)
    k2 = kp * (1.0 + (ap - 1.0) * kaw)
    e_in = jnp.exp(clp)
    e_ex = jnp.exp(clp - ldp)
    e_neg = jnp.exp(-clp)
    e_c = e_in[:, C - 1:C, :]
    at = -kk * e_ex
    rt = rp * e_in
    bt = kk * ap * e_neg
    kt = k2 * e_neg
    hbd = st_ref[...]

    a_s, r_s, b_s, k_s, v_s = stack(bf(at)), stack(bf(rt)), stack(bf(bt)), stack(bf(kt)), stack(bf(vp))
    ar = jnp.concatenate([a_s, r_s], axis=1)
    if (2 * C) % LANES == 0:
        m = _mm(ar, jnp.concatenate([b_s, k_s], axis=1), _BNT)
        m_ab, m_ak = m[:, :2 * C, :2 * C], m[:, :2 * C, 2 * C:]
        m_rb, m_rk = m[:, 2 * C:, :2 * C], m[:, 2 * C:, 2 * C:]
    else:
        m_ab, m_ak = _mm(a_s, b_s, _BNT), _mm(a_s, k_s, _BNT)
        m_rb, m_rk = _mm(r_s, b_s, _BNT), _mm(r_s, k_s, _BNT)
    xpow = bf(jnp.where(strict, m_ab, 0.0))
    m_ak = bf(jnp.where(strict, m_ak, 0.0))
    m_rb = bf(jnp.where(incl, m_rb, 0.0))
    m_rk = bf(jnp.where(incl, m_rk, 0.0))

    arh = _mm(ar, bf(hbd), _BNN)
    arh_y = arh[:, 2 * C:]
    u_s = arh[:, :2 * C] + _mm(m_ak, v_s, _BNN)
    n = 1
    while n < C:
        u_s = u_s + _mm(xpow, bf(u_s), _BNN)
        n *= 2
        if n < C:
            xpow = bf(_mm(xpow, xpow, _BNN))
    if (2 * C) % LANES == 0:
        y_s = arh_y + _mm(jnp.concatenate([m_rb, m_rk], axis=2), jnp.concatenate([bf(u_s), v_s], axis=1), _BNN)
    else:
        y_s = arh_y + (_mm(m_rb, bf(u_s), _BNN) + _mm(m_rk, v_s, _BNN))
    y = y_s[:, :C] + y_s[:, C:]
    u = u_s[:, :C] + u_s[:, C:]
    dmat = jnp.where(eye, jnp.broadcast_to(e_c, (RW_PAIRS, LANES, LANES)), 0.0)
    lhs = jnp.concatenate([bt * e_c, kt * e_c, dmat], axis=1)
    rhs = jnp.concatenate([u, vp, hbd], axis=1)
    st_ref[...] = jnp.where(blockdiag, _dot_hi(lhs, rhs, _BTN), 0.0)

    mu_y = head_sum(y) * (1.0 / RW_HEAD)
    dy = y - mu_y
    var = head_sum(dy * dy) * (1.0 / RW_HEAD)
    yn = dy * lax.rsqrt(var + RW_GN_EPS) * gng + gnb
    bonus = head_sum(rp * k2 * rkw) * vp
    yo = ((yn + bonus) * gp).astype(BF16)
    out_ref[...] = jnp.concatenate([yo[p] for p in range(RW_PAIRS)], axis=1)

    @pl.when(c == n_c - 1)
    def _():
        hfin_ref[...] = st_ref[...]


def _cumsum_rows(x, tri_bf16):
    xh, xm, xl = _split3(x)
    return _mm(tri_bf16, xh) + (_mm(tri_bf16, xm) + _mm(tri_bf16, xl))


def _rwkv(rw, shift0, h0, w, *, n_batch, n_chunks, chunk, row_blocks_per_batch):
    body = functools.partial(_rwkv_body, chunk=chunk)
    pp = lambda: _const_spec((RW_PAIRS, 1, LANES))
    return pl.pallas_call(
        body, grid=(n_batch, n_chunks),
        in_specs=[pl.BlockSpec((chunk, SHIFT_COLS), lambda b, c: (b * row_blocks_per_batch + c, 0)),
                  pl.BlockSpec((None, 1, SHIFT_COLS), lambda b, c: (b, 0, 0)),
                  pl.BlockSpec((None, RW_PAIRS, LANES, LANES), lambda b, c: (b, 0, 0, 0)),
                  _const_spec((1, SHIFT_COLS)), _const_spec((1, D_MODEL)), _const_spec((LORA_W, D_MODEL)),
                  _const_spec((LORA_W, D_MODEL)),
                  _const_spec((1, D_MODEL)), _const_spec((LORA_A, D_MODEL)), _const_spec((LORA_G, D_MODEL)),
                  pp(), pp(), pp(), pp(), pp()],
        out_specs=[pl.BlockSpec((chunk, D_MODEL), lambda b, c: (b * n_chunks + c, 0)),
                   pl.BlockSpec((None, RW_PAIRS, LANES, LANES), lambda b, c: (b, 0, 0, 0))],
        out_shape=[jax.ShapeDtypeStruct((n_batch * n_chunks * chunk, D_MODEL), BF16),
                   jax.ShapeDtypeStruct((n_batch, RW_PAIRS, LANES, LANES), F32)],
        scratch_shapes=[pltpu.VMEM((1, SHIFT_COLS), F32), pltpu.VMEM((RW_PAIRS, LANES, LANES), F32)],
        compiler_params=pltpu.CompilerParams(dimension_semantics=("arbitrary", "arbitrary"), vmem_limit_bytes=VMEM_LIMIT),
        name="rwkv",
    )(rw, shift0, h0, w["rw_mu"], w["rw_w0"], w["rw_w2h"], w["rw_w2m"], w["rw_a0"], w["rw_a2"], w["rw_g2"],
      w["rw_k_k"], w["rw_k_a"], w["rw_r_k"], w["rw_gn_g"], w["rw_gn_b"])


def _merge_body(x_ref, oa_ref, orw_ref, g0_ref, b0_ref, wg_ref, woa_ref, wor_ref, wout_ref, g1_ref, b1_ref, x1_ref):
    h = _layernorm(x_ref[...], g0_ref[...], b0_ref[...])
    gates = jax.nn.sigmoid(_mm(h.astype(BF16), wg_ref[...]))
    o_attn = _mm(oa_ref[...], woa_ref[...])
    o_rw = _mm(orw_ref[...], wor_ref[...])
    mixed = _mm((gates[:, :D_MODEL] * o_attn + gates[:, D_MODEL:] * o_rw).astype(BF16), wout_ref[...])
    x1_ref[...] = _layernorm(ALPHA * h + mixed, g1_ref[...], b1_ref[...])


def _merge(x, oa, orw, w, tm):
    rows = x.shape[0]
    spec = lambda: pl.BlockSpec((tm, D_MODEL), lambda i: (i, 0))
    vec = lambda: _const_spec((1, D_MODEL))
    sq = lambda: _const_spec((D_MODEL, D_MODEL))
    return pl.pallas_call(
        _merge_body, grid=(rows // tm,),
        in_specs=[spec(), spec(), spec(), vec(), vec(), _const_spec((D_MODEL, 2 * D_MODEL)), sq(), sq(), sq(), vec(), vec()],
        out_specs=spec(), out_shape=jax.ShapeDtypeStruct((rows, D_MODEL), F32),
        compiler_params=pltpu.CompilerParams(dimension_semantics=("arbitrary",), vmem_limit_bytes=VMEM_LIMIT),
        name="merge",
    )(x, oa, orw, w["ln0_g"], w["ln0_b"], w["wg"], w["w_o_attn"], w["w_o_rwkv"], w["w_out"], w["ln1_g"], w["ln1_b"])


def _ffn_body(x1_ref, wgate_ref, wup_ref, wdown_ref, g2_ref, b2_ref, y_ref):
    x1 = x1_ref[...]
    xb = x1.astype(BF16)
    hidden = jax.nn.silu(_mm(xb, wgate_ref[...])) * _mm(xb, wup_ref[...])
    ffn = _mm(hidden.astype(BF16), wdown_ref[...])
    y_ref[...] = _layernorm(ALPHA * x1 + ffn, g2_ref[...], b2_ref[...])


def _ffn(x1, w, tm):
    rows = x1.shape[0]
    spec = lambda: pl.BlockSpec((tm, D_MODEL), lambda i: (i, 0))
    return pl.pallas_call(
        _ffn_body, grid=(rows // tm,),
        in_specs=[spec(), _const_spec((D_MODEL, D_FF)), _const_spec((D_MODEL, D_FF)), _const_spec((D_FF, D_MODEL)),
                  _const_spec((1, D_MODEL)), _const_spec((1, D_MODEL))],
        out_specs=spec(), out_shape=jax.ShapeDtypeStruct((rows, D_MODEL), F32),
        compiler_params=pltpu.CompilerParams(dimension_semantics=("arbitrary",), vmem_limit_bytes=VMEM_LIMIT),
        name="ffn",
    )(x1, w["ffn_w_gate"], w["ffn_w_up"], w["ffn_w_down"], w["ln2_g"], w["ln2_b"])


def _rope_table(pos):
    pos = pos.astype(F32)[:, None]

    def one(rot, period):
        half = rot // 2
        inv = ROPE_THETA ** (-jnp.arange(0, rot, 2, dtype=F32) / rot)
        ang = pos * inv[None]
        cos, sin = jnp.cos(ang), jnp.sin(ang)
        t = pos.shape[0]
        ones = jnp.ones((t, period - rot), F32)
        zeros = jnp.zeros((t, period - rot), F32)
        c = jnp.concatenate([cos, cos, ones], axis=1)
        s = jnp.concatenate([-sin, sin, zeros], axis=1)
        reps = LANES // period
        return jnp.tile(c, (1, reps)), jnp.tile(s, (1, reps))

    ca, sa = one(ROPE_DIM_A, HEAD_DIM)
    ci, si = one(ROPE_DIM_I, HEAD_DIM_I)
    return jnp.concatenate([ca, sa, ci, si], axis=1)


def _pairs(vec):
    return vec.reshape(RW_PAIRS, 1, LANES).astype(F32)


def _state_to_pairs(s):
    b = s.shape[0]
    ht = jnp.swapaxes(s, -1, -2).reshape(b, RW_PAIRS, 2, RW_HEAD, RW_HEAD)
    z = jnp.zeros((b, RW_PAIRS, RW_HEAD, RW_HEAD), s.dtype)
    top = jnp.concatenate([ht[:, :, 0], z], axis=-1)
    bot = jnp.concatenate([z, ht[:, :, 1]], axis=-1)
    return jnp.concatenate([top, bot], axis=-2)


def _pairs_to_state(h):
    b = h.shape[0]
    h0 = h[:, :, :RW_HEAD, :RW_HEAD]
    h1 = h[:, :, RW_HEAD:, RW_HEAD:]
    ht = jnp.stack([h0, h1], axis=2).reshape(b, RW_HEADS, RW_HEAD, RW_HEAD)
    return jnp.swapaxes(ht, -1, -2)


def _pad_rows(a, rows, axis=1):
    pad = [(0, 0)] * a.ndim
    pad[axis] = (0, rows - a.shape[axis])
    return jnp.pad(a, pad)


def kernel(x_prompt, x_sample, cache_k, cache_v, cache_idx_k, state_wkv, state_shift, meta_tokens, ln0_g, ln0_b, w_in, idx_k_ln_g, idx_k_ln_b, rw_mu, rw_w0, rw_w2, rw_a0, rw_a2, rw_g2, rw_k_k, rw_k_a, rw_r_k, rw_gn_g, rw_gn_b, w_o_attn, w_o_rwkv, w_out, ln1_g, ln1_b, ffn_w_gate, ffn_w_up, ffn_w_down, ln2_g, ln2_b):
    B, S, _ = x_prompt.shape
    Bs, Ts, _ = x_sample.shape
    P = cache_k.shape[2] - N_META
    depth = w_in.shape[0]
    assert depth == 1 and S % (2 * CHUNK) == 0 and P % KEY_TILE == 0 and Ts == CHUNK
    l = 0
    row = lambda v: v.reshape(1, -1).astype(F32)
    win = w_in[l]
    wi_pack = jnp.concatenate([win[:, _QI0:_G0], jnp.zeros((D_MODEL, 640 - (_G0 - _QI0)), F32)], axis=1)
    lane_pad = lambda v: jnp.concatenate([v, jnp.zeros((LANES - HEAD_DIM_I,), F32)]).reshape(1, LANES)
    w = {
        "ln0_g": row(ln0_g), "ln0_b": row(ln0_b),
        "wa": win[:, _Q0:_QI0].astype(BF16), "wi": wi_pack.astype(BF16), "wr": win[:, _RW0:].astype(BF16),
        "wg": win[:, _G0:_RW0].astype(BF16),
        "ikg": lane_pad(idx_k_ln_g[l]), "ikb": lane_pad(idx_k_ln_b[l]),
        "rw_mu": row(rw_mu[l]), "rw_w0": row(rw_w0[l]), "rw_a0": row(rw_a0[l]),
        "rw_w2h": rw_w2[l].astype(BF16), "rw_w2m": (rw_w2[l] - rw_w2[l].astype(BF16).astype(F32)).astype(BF16),
        "rw_a2": rw_a2[l].astype(BF16), "rw_g2": rw_g2[l].astype(BF16),
        "rw_k_k": _pairs(rw_k_k[l]), "rw_k_a": _pairs(rw_k_a[l]), "rw_r_k": _pairs(rw_r_k[l]),
        "rw_gn_g": _pairs(rw_gn_g[l]), "rw_gn_b": _pairs(rw_gn_b[l]),
        "w_o_attn": w_o_attn[l].astype(BF16), "w_o_rwkv": w_o_rwkv[l].astype(BF16), "w_out": w_out[l].astype(BF16),
        "ln1_g": row(ln1_g[l]), "ln1_b": row(ln1_b[l]),
        "ffn_w_gate": ffn_w_gate[l].astype(BF16), "ffn_w_up": ffn_w_up[l].astype(BF16),
        "ffn_w_down": ffn_w_down[l].astype(BF16), "ln2_g": row(ln2_g[l]), "ln2_b": row(ln2_b[l]),
    }

    tm = 256
    x_meta = _pad_rows(meta_tokens.astype(F32), KEY_TILE, axis=0)
    m_q, m_k, m_v, m_kbf, m_vt, m_qi, m_kiwi, m_kibf, m_rw = _proj(
        x_meta, _rope_table(jnp.arange(KEY_TILE)), 1, KEY_TILE, w)
    xf = x_prompt.reshape(B * S, D_MODEL)
    f_q, f_k, f_v, f_kbf, f_vt, f_qi, f_kiwi, f_kibf, f_rw = _proj(
        xf, _rope_table(N_META + jnp.arange(S)), S // tm, tm, w)
    xs_pad = _pad_rows(x_sample, LANES, axis=1).reshape(Bs * LANES, D_MODEL)
    s_q, s_k, s_v, s_kbf, s_vt, s_qi, s_kiwi, s_kibf, s_rw = _proj(
        xs_pad, _rope_table(N_META + P + jnp.arange(LANES)), 1, LANES, w)

    def key_rows(meta_tile, frames, total):
        b = frames.shape[0]
        mt = jnp.broadcast_to(meta_tile[None], (b,) + meta_tile.shape)
        return _pad_rows(jnp.concatenate([mt, frames], axis=1), total, axis=1)

    sx_p = -(-(KEY_TILE + S) // SCORE_ROWS) * SCORE_ROWS
    o_attn_p = _dsa(
        f_q, f_qi, f_kiwi,
        key_rows(m_kbf, f_kbf.reshape(B, S, HEAD_DIM), sx_p),
        key_rows(m_kibf, f_kibf.reshape(B, S, HEAD_DIM_I), sx_p),
        key_rows(m_vt, f_vt.reshape(B, S // KEY_TILE, HEAD_DIM, KEY_TILE), sx_p // KEY_TILE),
        n_batch=B, n_qblk=S // LANES, causal=True, n_sel=min(TOPK_MAX, S // 4), n_keys=S)

    n_keys_s = P + Ts
    sx_s = -(-(KEY_TILE + P + LANES) // SCORE_ROWS) * SCORE_ROWS
    meta_rows = lambda c: _pad_rows(c[l][:, :N_META], KEY_TILE, axis=1)
    past_vt = jnp.swapaxes(cache_v[l][:, N_META:].reshape(Bs, P // KEY_TILE, KEY_TILE, HEAD_DIM), -1, -2)
    meta_vt = jnp.swapaxes(meta_rows(cache_v), -1, -2)[:, None]
    kbf_s = _pad_rows(jnp.concatenate([meta_rows(cache_k).astype(BF16), cache_k[l][:, N_META:].astype(BF16),
                                       s_kbf.reshape(Bs, LANES, HEAD_DIM)], axis=1), sx_s)
    kibf_s = _pad_rows(jnp.concatenate([meta_rows(cache_idx_k).astype(BF16), cache_idx_k[l][:, N_META:].astype(BF16),
                                        s_kibf.reshape(Bs, LANES, HEAD_DIM_I)], axis=1), sx_s)
    vt_s = _pad_rows(jnp.concatenate([meta_vt.astype(BF16), past_vt.astype(BF16),
                                      s_vt.reshape(Bs, 1, HEAD_DIM, KEY_TILE)], axis=1), sx_s // KEY_TILE)
    o_attn_s = _dsa(s_q, s_qi, s_kiwi, kbf_s, kibf_s, vt_s,
                    n_batch=Bs, n_qblk=1, causal=False, n_sel=min(TOPK_MAX, n_keys_s // 4), n_keys=n_keys_s)
    o_attn_s = o_attn_s.reshape(Bs, LANES, D_MODEL)[:, :Ts].reshape(Bs * Ts, D_MODEL)

    zero_shift = jnp.zeros((1, 1, SHIFT_COLS), F32)
    zero_state = jnp.zeros((1, RW_PAIRS, LANES, LANES), F32)
    _, h_meta = _rwkv(m_rw, zero_shift, zero_state, w, n_batch=1, n_chunks=1, chunk=N_META, row_blocks_per_batch=0)
    shift_meta = m_rw[N_META - 1:N_META].reshape(1, 1, SHIFT_COLS)
    o_rw_p, h_p = _rwkv(f_rw, jnp.broadcast_to(shift_meta, (B, 1, SHIFT_COLS)),
                        jnp.broadcast_to(h_meta, (B, RW_PAIRS, LANES, LANES)), w,
                        n_batch=B, n_chunks=S // CHUNK, chunk=CHUNK, row_blocks_per_batch=S // CHUNK)
    o_rw_s, h_s = _rwkv(s_rw, state_shift[l].reshape(Bs, 1, SHIFT_COLS), _state_to_pairs(state_wkv[l]), w,
                        n_batch=Bs, n_chunks=1, chunk=CHUNK, row_blocks_per_batch=LANES // CHUNK)

    y_p = _ffn(_merge(xf, o_attn_p, o_rw_p, w, tm), w, tm)
    xs_flat = x_sample.reshape(Bs * Ts, D_MODEL)
    tm_s = min(tm, Bs * Ts)
    y_s = _ffn(_merge(xs_flat, o_attn_s, o_rw_s, w, tm_s), w, tm_s)

    def with_meta(meta_rows_, frames, width):
        mt = jnp.broadcast_to(meta_rows_[None, :N_META, :width], (B, N_META, width))
        return jnp.concatenate([mt, frames.reshape(B, S, -1)[:, :, :width]], axis=1)[None]

    take = lambda a, width: a.reshape(Bs, LANES, -1)[:, :Ts, :width][None]
    return (
        y_p.reshape(B, S, D_MODEL),
        y_s.reshape(Bs, Ts, D_MODEL),
        with_meta(m_k, f_k, HEAD_DIM),
        with_meta(m_v, f_v, HEAD_DIM),
        with_meta(m_kiwi, f_kiwi, HEAD_DIM_I),
        _pairs_to_state(h_p)[None],
        f_rw.reshape(B, S, SHIFT_COLS)[:, -1][None],
        take(s_k, HEAD_DIM),
        take(s_v, HEAD_DIM),
        take(s_kiwi, HEAD_DIM_I),
        _pairs_to_state(h_s)[None],
        s_rw.reshape(Bs, LANES, SHIFT_COLS)[:, Ts - 1][None],
    )
```

```python
import functools

import numpy as np
import jax
import jax.numpy as jnp
from jax import lax
from jax.experimental import pallas as pl
from jax.experimental.pallas import tpu as pltpu

F32 = jnp.float32
BF16 = jnp.bfloat16
I32 = jnp.int32

D_MODEL = 1024
N_META = 16
CHUNK = 64
N_HEADS_A = 8
HEAD_DIM = 128
ROPE_DIM_A = HEAD_DIM // 4
N_HEADS_I = 8
HEAD_DIM_I = 64
ROPE_DIM_I = HEAD_DIM_I // 4
TOPK_MAX = 256
ROPE_THETA = 500000.0
RW_HEAD = 64
RW_HEADS = 16
RW_PAIRS = RW_HEADS // 2
LORA_W = 64
LORA_A = 64
LORA_G = 128
RW_GN_EPS = 64e-5
D_FF = 2816
LN_EPS = 1e-5
ALPHA = 2.0 ** 0.25
NEG = -1e30
LOG2_E = 1.4426950408889634
SHIFT_COLS = 3 * D_MODEL + LORA_W + LORA_A + LORA_G
_Q0, _K0, _V0, _QI0, _KI0, _WI0, _G0, _RW0 = 0, 1024, 1152, 1280, 1792, 1856, 1864, 3912
_IN_COLS = _RW0 + SHIFT_COLS

LANES = 128
KEY_TILE = 128
SCORE_ROWS = 512
ATT_ROWS = 256
RW_STREAMS = 2
INT_MIN = -2 ** 31
VMEM_LIMIT = 56 * 1024 * 1024

_NN = (((1,), (0,)), ((), ()))
_NT = (((1,), (1,)), ((), ()))
_TN = (((0,), (0,)), ((), ()))
_BNN = (((2,), (1,)), ((0,), (0,)))
_BNT = (((2,), (2,)), ((0,), (0,)))
_BTN = (((1,), (1,)), ((0,), (0,)))


def _mm(a, b, dn=_NN):
    return lax.dot_general(a, b, dn, preferred_element_type=F32)


def _split3(a):
    hi = a.astype(BF16)
    r1 = a - hi.astype(F32)
    mid = r1.astype(BF16)
    lo = (r1 - mid.astype(F32)).astype(BF16)
    return hi, mid, lo


def _dot_hi(a, b, dn=_NN):
    ah, am, _ = _split3(a)
    bh, bm, _ = _split3(b)
    return _mm(ah, bh, dn) + (_mm(ah, bm, dn) + _mm(am, bh, dn))


def _dot_exact_rhs(a, b_bf16, dn=_NN):
    ah = a.astype(BF16)
    am = (a - ah.astype(F32)).astype(BF16)
    return _mm(ah, b_bf16, dn) + _mm(am, b_bf16, dn)


def _layernorm(x, g, b):
    mu = jnp.mean(x, axis=-1, keepdims=True)
    d = x - mu
    var = jnp.mean(d * d, axis=-1, keepdims=True)
    return d * lax.rsqrt(var + LN_EPS) * g + b


def _const_spec(shape):
    nd = len(shape)
    return pl.BlockSpec(shape, lambda *_: (0,) * nd, pipeline_mode=pl.Buffered(1))


def _rope(x, cos, sin, period, half):
    width = x.shape[1]
    reps = width // LANES
    cc = jnp.concatenate([cos] * reps, axis=1) if reps > 1 else cos
    ss = jnp.concatenate([sin] * reps, axis=1) if reps > 1 else sin
    lane = lax.broadcasted_iota(I32, (1, width), 1) % period
    fwd = pltpu.roll(x, width - half, 1)
    bwd = pltpu.roll(x, half, 1)
    return x * cc + jnp.where(lane < half, fwd, bwd) * ss


def _proj_body(x_ref, tab_ref, g0_ref, b0_ref, wa_ref, wi_ref, wr_ref, ikg_ref, ikb_ref,
               q_ref, k_ref, v_ref, kbf_ref, vt_ref, qi_ref, kiwi_ref, kibf_ref, rw_ref):
    tm = x_ref.shape[0]
    hb = _layernorm(x_ref[...], g0_ref[...], b0_ref[...]).astype(BF16)
    tab = tab_ref[...]
    cos_a, sin_a = tab[:, 0:128], tab[:, 128:256]
    cos_i, sin_i = tab[:, 256:384], tab[:, 384:512]

    pa = _mm(hb, wa_ref[...])
    q = _rope(pa[:, :1024], cos_a, sin_a, HEAD_DIM, ROPE_DIM_A // 2)
    q_ref[...] = (q * (HEAD_DIM ** -0.5 * LOG2_E)).astype(BF16)
    k = _rope(pa[:, 1024:1152], cos_a, sin_a, HEAD_DIM, ROPE_DIM_A // 2)
    k_ref[...] = k
    kbf_ref[...] = k.astype(BF16)
    v = pa[:, 1152:1280]
    v_ref[...] = v
    for j in range(tm // KEY_TILE):
        vt_ref[j] = v[j * KEY_TILE:(j + 1) * KEY_TILE, :].T.astype(BF16)

    pi = _mm(hb, wi_ref[...])
    qi_ref[...] = _rope(pi[:, :512], cos_i, sin_i, HEAD_DIM_I, ROPE_DIM_I // 2).astype(BF16)
    kw = pi[:, 512:640]
    lane = lax.broadcasted_iota(I32, (1, LANES), 1)
    is_ki = lane < HEAD_DIM_I
    mu = jnp.sum(jnp.where(is_ki, kw, 0.0), axis=-1, keepdims=True) * (1.0 / HEAD_DIM_I)
    d = jnp.where(is_ki, kw - mu, 0.0)
    var = jnp.sum(d * d, axis=-1, keepdims=True) * (1.0 / HEAD_DIM_I)
    ki = d * lax.rsqrt(var + LN_EPS) * ikg_ref[...] + ikb_ref[...]
    ki = _rope(ki, cos_i, sin_i, HEAD_DIM_I, ROPE_DIM_I // 2)
    wi_scale = (N_HEADS_I ** -0.5) * (HEAD_DIM_I ** -0.5)
    kiwi_ref[...] = jnp.where(is_ki, ki, jnp.where(lane < HEAD_DIM_I + N_HEADS_I, kw * wi_scale, 0.0))
    kibf_ref[...] = ki[:, :HEAD_DIM_I].astype(BF16)

    rw_ref[...] = _mm(hb, wr_ref[...])


def _proj(x, tab, tab_blocks, tm, w):
    rows = x.shape[0]
    n = rows // tm
    row_spec = lambda width: pl.BlockSpec((tm, width), lambda i: (i, 0))
    out_shape = [
        jax.ShapeDtypeStruct((rows, 1024), BF16),
        jax.ShapeDtypeStruct((rows, HEAD_DIM), F32),
        jax.ShapeDtypeStruct((rows, HEAD_DIM), F32),
        jax.ShapeDtypeStruct((rows, HEAD_DIM), BF16),
        jax.ShapeDtypeStruct((rows // KEY_TILE, HEAD_DIM, KEY_TILE), BF16),
        jax.ShapeDtypeStruct((rows, 512), BF16),
        jax.ShapeDtypeStruct((rows, LANES), F32),
        jax.ShapeDtypeStruct((rows, HEAD_DIM_I), BF16),
        jax.ShapeDtypeStruct((rows, SHIFT_COLS), F32),
    ]
    out_specs = [row_spec(1024), row_spec(HEAD_DIM), row_spec(HEAD_DIM), row_spec(HEAD_DIM),
                 pl.BlockSpec((tm // KEY_TILE, HEAD_DIM, KEY_TILE), lambda i: (i, 0, 0)),
                 row_spec(512), row_spec(LANES), row_spec(HEAD_DIM_I), row_spec(SHIFT_COLS)]
    in_specs = [row_spec(D_MODEL),
                pl.BlockSpec((tm, 512), lambda i: (i % tab_blocks, 0)),
                _const_spec((1, D_MODEL)), _const_spec((1, D_MODEL)),
                _const_spec((D_MODEL, 1280)), _const_spec((D_MODEL, 640)), _const_spec((D_MODEL, SHIFT_COLS)),
                _const_spec((1, LANES)), _const_spec((1, LANES))]
    return pl.pallas_call(
        _proj_body, grid=(n,), in_specs=in_specs, out_specs=out_specs, out_shape=out_shape,
        compiler_params=pltpu.CompilerParams(dimension_semantics=("arbitrary",), vmem_limit_bytes=VMEM_LIMIT),
        name="proj",
    )(x, tab, w["ln0_g"], w["ln0_b"], w["wa"], w["wi"], w["wr"], w["ikg"], w["ikb"])


def _dsa_body(q_ref, qi_ref, kiwi_ref, kbf_ref, kibf_ref, vt_ref, o_ref,
              sc_ref, bias_ref, m_ref, l_ref, acc_ref, s_ref, *, causal, n_sel, n_keys):
    i = pl.program_id(1)
    if causal:
        lim_lo = KEY_TILE + i * (2 * CHUNK) + CHUNK
        lim_hi = lim_lo + CHUNK
    else:
        lim_lo = KEY_TILE + n_keys
        lim_hi = lim_lo
    n_score = (lim_hi + SCORE_ROWS - 1) // SCORE_ROWS

    heads_on_lanes = lambda t, n, rows: jnp.concatenate([t[h * rows:(h + 1) * rows, :] for h in range(n)], axis=1)
    qi_all = heads_on_lanes(qi_ref[...].astype(F32).T.astype(BF16), N_HEADS_I, HEAD_DIM_I)
    w_all = heads_on_lanes(kiwi_ref[...].T[HEAD_DIM_I:HEAD_DIM_I + N_HEADS_I, :], N_HEADS_I, 1)
    lane = lax.broadcasted_iota(I32, (1, LANES), 1)

    def score_block(j, carry):
        r0 = pl.multiple_of(j * SCORE_ROWS, SCORE_ROWS)
        s = jnp.maximum(_mm(kibf_ref[pl.ds(r0, SCORE_ROWS), :], qi_all), 0.0) * w_all
        sc = s[:, :LANES]
        for h in range(1, N_HEADS_I):
            sc = sc + s[:, h * LANES:(h + 1) * LANES]
        row = r0 + lax.broadcasted_iota(I32, (SCORE_ROWS, 1), 0)
        adm = (row < N_META) | ((row >= KEY_TILE) & (row < lim_lo)) | ((row >= lim_lo) & (row < lim_hi) & (lane >= CHUNK))
        sc_ref[pl.ds(r0, SCORE_ROWS), :] = jnp.where(adm, sc, NEG)
        return carry

    lax.fori_loop(0, n_score, score_block, 0)
    tiles_per_blk = SCORE_ROWS // KEY_TILE
    n_tiles = n_score * tiles_per_blk

    def count(pred):
        def body(j, acc):
            for u in range(tiles_per_blk):
                r0 = pl.multiple_of(j * SCORE_ROWS + u * KEY_TILE, KEY_TILE)
                m = pred(sc_ref[pl.ds(r0, KEY_TILE), :], r0)
                acc = jnp.where(m.reshape(KEY_TILE // 8, 8, LANES), acc + 1, acc)
            return acc
        acc = lax.fori_loop(0, n_score, body, jnp.zeros((KEY_TILE // 8, 8, LANES), I32))
        return acc.sum(axis=0).sum(axis=0, keepdims=True)

    def as_score(k):
        return lax.bitcast_convert_type(k ^ ((k >> 31) & 0x7FFFFFFF), F32)

    def bit_step(b, st):
        t, cnt_t = st
        cand = t + lax.shift_left(jnp.int32(1), 31 - b)
        cand_f = as_score(cand)
        cnt = count(lambda tile, r0: tile >= cand_f)
        take = cnt >= n_sel
        return jnp.where(take, cand, t), jnp.where(take, cnt, cnt_t)

    all_rows = jnp.zeros((1, LANES), I32) + n_score * SCORE_ROWS
    thr_i, cnt_thr = lax.fori_loop(0, 32, bit_step, (jnp.full((1, LANES), INT_MIN, I32), all_rows))
    thr = as_score(thr_i)
    big = jnp.full((1, LANES), 2 ** 30, I32)

    def tie_path():
        cnt_gt = count(lambda tile, r0: tile > thr)
        need = n_sel - cnt_gt
        tied = cnt_thr - cnt_gt > need

        def tie_search():
            def rbit(b, r):
                cand = r + lax.shift_left(jnp.int32(1), 14 - b)
                cnt = count(lambda tile, r0: (tile == thr) & ((r0 + lax.broadcasted_iota(I32, (KEY_TILE, 1), 0)) < cand))
                return jnp.where(cnt <= need, cand, r)
            r = lax.fori_loop(0, 15, rbit, jnp.zeros((1, LANES), I32))
            return jnp.where(tied, r, big)

        return lax.cond(jnp.max(tied.astype(I32)) > 0, tie_search, lambda: big)

    row_lim = lax.cond(jnp.max(jnp.abs(cnt_thr - n_sel)) > 0, tie_path, lambda: big)

    def bias_block(j, carry):
        r0 = pl.multiple_of(j * KEY_TILE, KEY_TILE)
        tile = sc_ref[pl.ds(r0, KEY_TILE), :]
        row = r0 + lax.broadcasted_iota(I32, (KEY_TILE, 1), 0)
        sel = ((tile > thr) | ((tile == thr) & (row < row_lim))) & (tile > 0.5 * NEG)
        bias_ref[pl.ds(r0, KEY_TILE), :] = jnp.where(sel, 0.0, NEG)
        return carry

    lax.fori_loop(0, n_tiles, bias_block, 0)

    q_all = heads_on_lanes(q_ref[...].astype(F32).T.astype(BF16), N_HEADS_A, HEAD_DIM)
    m_ref[...] = jnp.full(m_ref.shape, NEG, F32)
    l_ref[...] = jnp.zeros(l_ref.shape, F32)
    acc_ref[...] = jnp.zeros(acc_ref.shape, F32)
    tiles_per_step = ATT_ROWS // KEY_TILE
    ones_rows = jnp.ones((16, ATT_ROWS), BF16)
    pair_w = 2 * LANES

    n_att = n_score * (SCORE_ROWS // ATT_ROWS)

    def logits(j, slot):
        r0 = pl.multiple_of(j * ATT_ROWS, ATT_ROWS)
        s_ref[slot] = _mm(kbf_ref[pl.ds(r0, ATT_ROWS), :], q_all)

    def att_block(j, slot):
        logits(jnp.minimum(j + 1, n_att - 1), 1 - slot)
        r0 = pl.multiple_of(j * ATT_ROWS, ATT_ROWS)
        vtb = jnp.concatenate([vt_ref[j * tiles_per_step + u] for u in range(tiles_per_step)], axis=1)
        vtb = jnp.concatenate([vtb, ones_rows], axis=0)
        bias = bias_ref[pl.ds(r0, ATT_ROWS), :]
        s = s_ref[slot] + jnp.concatenate([bias] * N_HEADS_A, axis=1)
        m_old = m_ref[...]
        m_new = jnp.maximum(m_old, jnp.max(s, axis=0, keepdims=True))
        alpha = jnp.exp2(m_old - m_new)
        p = jnp.exp2(s - m_new).astype(BF16)
        m_ref[...] = m_new
        for pr in range(N_HEADS_A // 2):
            cols = slice(pr * pair_w, (pr + 1) * pair_w)
            pv = _mm(vtb, p[:, cols])
            l_ref[:, cols] = alpha[:, cols] * l_ref[:, cols] + pv[HEAD_DIM:HEAD_DIM + 1, :]
            acc_ref[:, cols] = alpha[:, cols] * acc_ref[:, cols] + pv[:HEAD_DIM, :]

    def att_two(jj, carry):
        att_block(2 * jj, 0)
        att_block(2 * jj + 1, 1)
        return carry

    logits(0, 0)
    lax.fori_loop(0, n_att // 2, att_two, 0)
    out_t = acc_ref[...] * (1.0 / l_ref[...])
    o_ref[...] = jnp.concatenate([out_t[:, h * LANES:(h + 1) * LANES].T for h in range(N_HEADS_A)], axis=1).astype(BF16)


def _dsa(q, qi, kiwi, kbf, kibf, vt, *, n_batch, n_qblk, causal, n_sel, n_keys):
    sx = kbf.shape[1]
    qmap = lambda b, i: (b * n_qblk + i, 0)
    body = functools.partial(_dsa_body, causal=causal, n_sel=n_sel, n_keys=n_keys)
    return pl.pallas_call(
        body, grid=(n_batch, n_qblk),
        in_specs=[pl.BlockSpec((LANES, 1024), qmap), pl.BlockSpec((LANES, 512), qmap), pl.BlockSpec((LANES, LANES), qmap),
                  pl.BlockSpec((None, sx, HEAD_DIM), lambda b, i: (b, 0, 0)),
                  pl.BlockSpec((None, sx, HEAD_DIM_I), lambda b, i: (b, 0, 0)),
                  pl.BlockSpec((None, sx // KEY_TILE, HEAD_DIM, KEY_TILE), lambda b, i: (b, 0, 0, 0))],
        out_specs=pl.BlockSpec((LANES, 1024), qmap),
        out_shape=jax.ShapeDtypeStruct((n_batch * n_qblk * LANES, 1024), BF16),
        scratch_shapes=[pltpu.VMEM((sx, LANES), F32), pltpu.VMEM((sx, LANES), F32),
                        pltpu.VMEM((1, N_HEADS_A * LANES), F32), pltpu.VMEM((1, N_HEADS_A * LANES), F32),
                        pltpu.VMEM((HEAD_DIM, N_HEADS_A * LANES), F32),
                        pltpu.VMEM((2, ATT_ROWS, N_HEADS_A * LANES), F32)],
        compiler_params=pltpu.CompilerParams(dimension_semantics=("arbitrary", "arbitrary"), vmem_limit_bytes=VMEM_LIMIT),
        name="dsa",
    )(q, qi, kiwi, kbf, kibf, vt)


def _rwkv_body(rw_ref, shift0_ref, h0_ref, mu_ref, w0_ref, w2h_ref, w2m_ref, a0_ref, a2_ref, g2_ref,
               kk_ref, ka_ref, rk_ref, gng_ref, gnb_ref,
               out_ref, hfin_ref,
               prev_ref, st_ref, *, chunk, streams):
    c = pl.program_id(1)
    n_c = pl.num_programs(1)
    C = chunk
    P = streams * RW_PAIRS

    @pl.when(c == 0)
    def _():
        prev_ref[...] = shift0_ref[...]
        st_ref[...] = h0_ref[...].reshape(P, LANES, LANES)

    x = rw_ref[...].reshape(streams * C, SHIFT_COLS)
    rowi = lax.broadcasted_iota(I32, (streams * C, 1), 0)
    last = jnp.concatenate([jnp.broadcast_to(prev_ref[s], (C, SHIFT_COLS)) for s in range(streams)], axis=0)
    prev = jnp.where(rowi % C == 0, last, pltpu.roll(x, 1, 0))
    for s in range(streams):
        prev_ref[s] = x[(s + 1) * C - 1:(s + 1) * C, :]
    xs = x + (prev - x) * mu_ref[...]
    r = xs[:, 0:1024]
    k = xs[:, 1024:2048]
    v = xs[:, 2048:3072]
    lw = xs[:, 3072:3072 + LORA_W]
    la = xs[:, 3072 + LORA_W:3072 + LORA_W + LORA_A]
    lg = xs[:, 3072 + LORA_W + LORA_A:SHIFT_COLS]
    tw = jnp.tanh(lw)
    twh = tw.astype(BF16)
    twm = (tw - twh.astype(F32)).astype(BF16)
    wx = w0_ref[...] + (_mm(twh, w2h_ref[...]) + (_mm(twh, w2m_ref[...]) + _mm(twm, w2h_ref[...])))
    w_log = -jax.nn.softplus(-wx) - 0.5
    ld = -jnp.exp(w_log)
    a = jax.nn.sigmoid(a0_ref[...] + _mm(la.astype(BF16), a2_ref[...]))
    g = _mm(jax.nn.sigmoid(lg).astype(BF16), g2_ref[...])
    tr = lax.broadcasted_iota(I32, (streams * C, streams * C), 0)
    tc = lax.broadcasted_iota(I32, (streams * C, streams * C), 1)
    tri = ((tr >= tc) & (tr // C == tc // C)).astype(BF16)
    cl = _cumsum_rows(ld, tri)

    def pairs(t):
        return jnp.stack([t[s * C:(s + 1) * C, p * LANES:(p + 1) * LANES]
                          for s in range(streams) for p in range(RW_PAIRS)], axis=0)

    rp, kp, vp, ap, ldp, clp, gp = pairs(r), pairs(k), pairs(v), pairs(a), pairs(ld), pairs(cl), pairs(g)
    per_stream = lambda ref: jnp.concatenate([ref[...]] * streams, axis=0)
    kkw, kaw, rkw, gng, gnb = (per_stream(ref) for ref in (kk_ref, ka_ref, rk_ref, gng_ref, gnb_ref))

    lane = lax.broadcasted_iota(I32, (1, 1, LANES), 2)
    lane_h0 = lane < RW_HEAD
    r2 = lax.broadcasted_iota(I32, (2 * C, 2 * C), 0)
    c2 = lax.broadcasted_iota(I32, (2 * C, 2 * C), 1)
    same_head = (r2 < C) == (c2 < C)
    strict = same_head & ((r2 % C) > (c2 % C))
    incl = same_head & ((r2 % C) >= (c2 % C))
    rl = lax.broadcasted_iota(I32, (LANES, LANES), 0)
    cl_ = lax.broadcasted_iota(I32, (LANES, LANES), 1)
    blockdiag = (rl < RW_HEAD) == (cl_ < RW_HEAD)
    eye = rl == cl_
    ones_bd = blockdiag.astype(BF16)

    def stack(t):
        return jnp.concatenate([jnp.where(lane_h0, t, 0.0), jnp.where(lane_h0, 0.0, t)], axis=1)

    def head_sum(t, split=False):
        t2 = t.reshape(P * C, LANES)
        s = _dot_exact_rhs(t2, ones_bd) if split else _mm(t2.astype(BF16), ones_bd)
        return s.reshape(P, C, LANES)

    bf = lambda t: t.astype(BF16)
    kk = kp * kkw
    kk = kk / jnp.maximum(jnp.sqrt(head_sum(kk * kk, split=True)), 1e-12)
    k2 = kp * (1.0 + (ap - 1.0) * kaw)
    e_in = jnp.exp(clp)
    e_ex = jnp.exp(clp - ldp)
    e_neg = jnp.exp(-clp)
    e_c = e_in[:, C - 1:C, :]
    at = -kk * e_ex
    rt = rp * e_in
    bt = kk * ap * e_neg
    kt = k2 * e_neg
    hbd = st_ref[...]

    a_s, r_s, b_s, k_s, v_s = stack(bf(at)), stack(bf(rt)), stack(bf(bt)), stack(bf(kt)), stack(bf(vp))
    ar = jnp.concatenate([a_s, r_s], axis=1)
    if (2 * C) % LANES == 0:
        m = _mm(ar, jnp.concatenate([b_s, k_s], axis=1), _BNT)
        m_ab, m_ak = m[:, :2 * C, :2 * C], m[:, :2 * C, 2 * C:]
        m_rb, m_rk = m[:, 2 * C:, :2 * C], m[:, 2 * C:, 2 * C:]
    else:
        m_ab, m_ak = _mm(a_s, b_s, _BNT), _mm(a_s, k_s, _BNT)
        m_rb, m_rk = _mm(r_s, b_s, _BNT), _mm(r_s, k_s, _BNT)
    xpow = bf(jnp.where(strict, m_ab, 0.0))
    m_ak = bf(jnp.where(strict, m_ak, 0.0))
    m_rb = bf(jnp.where(incl, m_rb, 0.0))
    m_rk = bf(jnp.where(incl, m_rk, 0.0))

    arh = _mm(ar, bf(hbd), _BNN)
    arh_y = arh[:, 2 * C:]
    u_s = arh[:, :2 * C] + _mm(m_ak, v_s, _BNN)
    n = 1
    while n < C:
        u_s = u_s + _mm(xpow, bf(u_s), _BNN)
        n *= 2
        if n < C:
            xpow = bf(_mm(xpow, xpow, _BNN))
    if (2 * C) % LANES == 0:
        y_s = arh_y + _mm(jnp.concatenate([m_rb, m_rk], axis=2), jnp.concatenate([bf(u_s), v_s], axis=1), _BNN)
    else:
        y_s = arh_y + (_mm(m_rb, bf(u_s), _BNN) + _mm(m_rk, v_s, _BNN))
    y = y_s[:, :C] + y_s[:, C:]
    u = u_s[:, :C] + u_s[:, C:]
    dmat = jnp.where(eye, jnp.broadcast_to(e_c, (P, LANES, LANES)), 0.0)
    lhs = jnp.concatenate([bt * e_c, kt * e_c, dmat], axis=1)
    rhs = jnp.concatenate([u, vp, hbd], axis=1)
    st_ref[...] = jnp.where(blockdiag, _mm(bf(lhs), bf(rhs), _BTN), 0.0)

    mu_y = head_sum(y) * (1.0 / RW_HEAD)
    dy = y - mu_y
    var = head_sum(dy * dy) * (1.0 / RW_HEAD)
    yn = dy * lax.rsqrt(var + RW_GN_EPS) * gng + gnb
    bonus = head_sum(rp * k2 * rkw) * vp
    yo = ((yn + bonus) * gp).astype(BF16)
    for s in range(streams):
        out_ref[s] = jnp.concatenate([yo[s * RW_PAIRS + p] for p in range(RW_PAIRS)], axis=1)

    @pl.when(c == n_c - 1)
    def _():
        hfin_ref[...] = st_ref[...].reshape(streams, RW_PAIRS, LANES, LANES)


def _cumsum_rows(x, tri_bf16):
    xh, xm, xl = _split3(x)
    return _mm(tri_bf16, xh) + (_mm(tri_bf16, xm) + _mm(tri_bf16, xl))


def _rwkv(rw, shift0, h0, w, *, n_chunks, chunk):
    n_batch = rw.shape[0]
    streams = RW_STREAMS if n_batch % RW_STREAMS == 0 else 1
    body = functools.partial(_rwkv_body, chunk=chunk, streams=streams)
    pp = lambda: _const_spec((RW_PAIRS, 1, LANES))
    return pl.pallas_call(
        body, grid=(n_batch // streams, n_chunks),
        in_specs=[pl.BlockSpec((streams, chunk, SHIFT_COLS), lambda b, c: (b, c, 0)),
                  pl.BlockSpec((streams, 1, SHIFT_COLS), lambda b, c: (b, 0, 0)),
                  pl.BlockSpec((streams, RW_PAIRS, LANES, LANES), lambda b, c: (b, 0, 0, 0)),
                  _const_spec((1, SHIFT_COLS)), _const_spec((1, D_MODEL)), _const_spec((LORA_W, D_MODEL)),
                  _const_spec((LORA_W, D_MODEL)),
                  _const_spec((1, D_MODEL)), _const_spec((LORA_A, D_MODEL)), _const_spec((LORA_G, D_MODEL)),
                  pp(), pp(), pp(), pp(), pp()],
        out_specs=[pl.BlockSpec((streams, chunk, D_MODEL), lambda b, c: (b, c, 0)),
                   pl.BlockSpec((streams, RW_PAIRS, LANES, LANES), lambda b, c: (b, 0, 0, 0))],
        out_shape=[jax.ShapeDtypeStruct((n_batch, n_chunks * chunk, D_MODEL), BF16),
                   jax.ShapeDtypeStruct((n_batch, RW_PAIRS, LANES, LANES), F32)],
        scratch_shapes=[pltpu.VMEM((streams, 1, SHIFT_COLS), F32), pltpu.VMEM((streams * RW_PAIRS, LANES, LANES), F32)],
        compiler_params=pltpu.CompilerParams(dimension_semantics=("arbitrary", "arbitrary"), vmem_limit_bytes=VMEM_LIMIT),
        name="rwkv",
    )(rw, shift0, h0, w["rw_mu"], w["rw_w0"], w["rw_w2h"], w["rw_w2m"], w["rw_a0"], w["rw_a2"], w["rw_g2"],
      w["rw_k_k"], w["rw_k_a"], w["rw_r_k"], w["rw_gn_g"], w["rw_gn_b"])


def _merge_body(x_ref, oa_ref, orw_ref, g0_ref, b0_ref, wg_ref, woa_ref, wor_ref, wout_ref, g1_ref, b1_ref, x1_ref):
    h = _layernorm(x_ref[...], g0_ref[...], b0_ref[...])
    gates = jax.nn.sigmoid(_mm(h.astype(BF16), wg_ref[...]))
    o_attn = _mm(oa_ref[...], woa_ref[...])
    o_rw = _mm(orw_ref[...], wor_ref[...])
    mixed = _mm((gates[:, :D_MODEL] * o_attn + gates[:, D_MODEL:] * o_rw).astype(BF16), wout_ref[...])
    x1_ref[...] = _layernorm(ALPHA * h + mixed, g1_ref[...], b1_ref[...])


def _merge(x, oa, orw, w, tm):
    rows = x.shape[0]
    spec = lambda: pl.BlockSpec((tm, D_MODEL), lambda i: (i, 0))
    vec = lambda: _const_spec((1, D_MODEL))
    sq = lambda: _const_spec((D_MODEL, D_MODEL))
    return pl.pallas_call(
        _merge_body, grid=(rows // tm,),
        in_specs=[spec(), spec(), spec(), vec(), vec(), _const_spec((D_MODEL, 2 * D_MODEL)), sq(), sq(), sq(), vec(), vec()],
        out_specs=spec(), out_shape=jax.ShapeDtypeStruct((rows, D_MODEL), F32),
        compiler_params=pltpu.CompilerParams(dimension_semantics=("arbitrary",), vmem_limit_bytes=VMEM_LIMIT),
        name="merge",
    )(x, oa, orw, w["ln0_g"], w["ln0_b"], w["wg"], w["w_o_attn"], w["w_o_rwkv"], w["w_out"], w["ln1_g"], w["ln1_b"])


def _ffn_body(x1_ref, wgate_ref, wup_ref, wdown_ref, g2_ref, b2_ref, y_ref):
    x1 = x1_ref[...]
    xb = x1.astype(BF16)
    hidden = jax.nn.silu(_mm(xb, wgate_ref[...])) * _mm(xb, wup_ref[...])
    ffn = _mm(hidden.astype(BF16), wdown_ref[...])
    y_ref[...] = _layernorm(ALPHA * x1 + ffn, g2_ref[...], b2_ref[...])


def _ffn(x1, w, tm):
    rows = x1.shape[0]
    spec = lambda: pl.BlockSpec((tm, D_MODEL), lambda i: (i, 0))
    return pl.pallas_call(
        _ffn_body, grid=(rows // tm,),
        in_specs=[spec(), _const_spec((D_MODEL, D_FF)), _const_spec((D_MODEL, D_FF)), _const_spec((D_FF, D_MODEL)),
                  _const_spec((1, D_MODEL)), _const_spec((1, D_MODEL))],
        out_specs=spec(), out_shape=jax.ShapeDtypeStruct((rows, D_MODEL), F32),
        compiler_params=pltpu.CompilerParams(dimension_semantics=("arbitrary",), vmem_limit_bytes=VMEM_LIMIT),
        name="ffn",
    )(x1, w["ffn_w_gate"], w["ffn_w_up"], w["ffn_w_down"], w["ln2_g"], w["ln2_b"])


def _rope_table(pos):
    pos = pos.astype(F32)[:, None]

    def one(rot, period):
        half = rot // 2
        inv = ROPE_THETA ** (-jnp.arange(0, rot, 2, dtype=F32) / rot)
        ang = pos * inv[None]
        cos, sin = jnp.cos(ang), jnp.sin(ang)
        t = pos.shape[0]
        ones = jnp.ones((t, period - rot), F32)
        zeros = jnp.zeros((t, period - rot), F32)
        c = jnp.concatenate([cos, cos, ones], axis=1)
        s = jnp.concatenate([-sin, sin, zeros], axis=1)
        reps = LANES // period
        return jnp.tile(c, (1, reps)), jnp.tile(s, (1, reps))

    ca, sa = one(ROPE_DIM_A, HEAD_DIM)
    ci, si = one(ROPE_DIM_I, HEAD_DIM_I)
    return jnp.concatenate([ca, sa, ci, si], axis=1)


def _pairs(vec):
    return vec.reshape(RW_PAIRS, 1, LANES).astype(F32)


def _state_to_pairs(s):
    b = s.shape[0]
    ht = jnp.swapaxes(s, -1, -2).reshape(b, RW_PAIRS, 2, RW_HEAD, RW_HEAD)
    z = jnp.zeros((b, RW_PAIRS, RW_HEAD, RW_HEAD), s.dtype)
    top = jnp.concatenate([ht[:, :, 0], z], axis=-1)
    bot = jnp.concatenate([z, ht[:, :, 1]], axis=-1)
    return jnp.concatenate([top, bot], axis=-2)


def _pairs_to_state(h):
    b = h.shape[0]
    h0 = h[:, :, :RW_HEAD, :RW_HEAD]
    h1 = h[:, :, RW_HEAD:, RW_HEAD:]
    ht = jnp.stack([h0, h1], axis=2).reshape(b, RW_HEADS, RW_HEAD, RW_HEAD)
    return jnp.swapaxes(ht, -1, -2)


def _pad_rows(a, rows, axis=1):
    pad = [(0, 0)] * a.ndim
    pad[axis] = (0, rows - a.shape[axis])
    return jnp.pad(a, pad)


def kernel(x_prompt, x_sample, cache_k, cache_v, cache_idx_k, state_wkv, state_shift, meta_tokens, ln0_g, ln0_b, w_in, idx_k_ln_g, idx_k_ln_b, rw_mu, rw_w0, rw_w2, rw_a0, rw_a2, rw_g2, rw_k_k, rw_k_a, rw_r_k, rw_gn_g, rw_gn_b, w_o_attn, w_o_rwkv, w_out, ln1_g, ln1_b, ffn_w_gate, ffn_w_up, ffn_w_down, ln2_g, ln2_b):
    B, S, _ = x_prompt.shape
    Bs, Ts, _ = x_sample.shape
    P = cache_k.shape[2] - N_META
    depth = w_in.shape[0]
    assert depth == 1 and S % (2 * CHUNK) == 0 and P % KEY_TILE == 0 and Ts == CHUNK
    l = 0
    row = lambda v: v.reshape(1, -1).astype(F32)
    win = w_in[l]
    wi_pack = jnp.concatenate([win[:, _QI0:_G0], jnp.zeros((D_MODEL, 640 - (_G0 - _QI0)), F32)], axis=1)
    lane_pad = lambda v: jnp.concatenate([v, jnp.zeros((LANES - HEAD_DIM_I,), F32)]).reshape(1, LANES)
    w = {
        "ln0_g": row(ln0_g), "ln0_b": row(ln0_b),
        "wa": win[:, _Q0:_QI0].astype(BF16), "wi": wi_pack.astype(BF16), "wr": win[:, _RW0:].astype(BF16),
        "wg": win[:, _G0:_RW0].astype(BF16),
        "ikg": lane_pad(idx_k_ln_g[l]), "ikb": lane_pad(idx_k_ln_b[l]),
        "rw_mu": row(rw_mu[l]), "rw_w0": row(rw_w0[l]), "rw_a0": row(rw_a0[l]),
        "rw_w2h": rw_w2[l].astype(BF16), "rw_w2m": (rw_w2[l] - rw_w2[l].astype(BF16).astype(F32)).astype(BF16),
        "rw_a2": rw_a2[l].astype(BF16), "rw_g2": rw_g2[l].astype(BF16),
        "rw_k_k": _pairs(rw_k_k[l]), "rw_k_a": _pairs(rw_k_a[l]), "rw_r_k": _pairs(rw_r_k[l]),
        "rw_gn_g": _pairs(rw_gn_g[l]), "rw_gn_b": _pairs(rw_gn_b[l]),
        "w_o_attn": w_o_attn[l].astype(BF16), "w_o_rwkv": w_o_rwkv[l].astype(BF16), "w_out": w_out[l].astype(BF16),
        "ln1_g": row(ln1_g[l]), "ln1_b": row(ln1_b[l]),
        "ffn_w_gate": ffn_w_gate[l].astype(BF16), "ffn_w_up": ffn_w_up[l].astype(BF16),
        "ffn_w_down": ffn_w_down[l].astype(BF16), "ln2_g": row(ln2_g[l]), "ln2_b": row(ln2_b[l]),
    }

    tm = 512
    x_meta = _pad_rows(meta_tokens.astype(F32), KEY_TILE, axis=0)
    m_q, m_k, m_v, m_kbf, m_vt, m_qi, m_kiwi, m_kibf, m_rw = _proj(
        x_meta, _rope_table(jnp.arange(KEY_TILE)), 1, KEY_TILE, w)
    xf = x_prompt.reshape(B * S, D_MODEL)
    f_q, f_k, f_v, f_kbf, f_vt, f_qi, f_kiwi, f_kibf, f_rw = _proj(
        xf, _rope_table(N_META + jnp.arange(S)), S // tm, tm, w)
    xs_pad = _pad_rows(x_sample, LANES, axis=1).reshape(Bs * LANES, D_MODEL)
    s_q, s_k, s_v, s_kbf, s_vt, s_qi, s_kiwi, s_kibf, s_rw = _proj(
        xs_pad, _rope_table(N_META + P + jnp.arange(LANES)), 1, LANES, w)

    def key_rows(meta_tile, frames, total):
        b = frames.shape[0]
        mt = jnp.broadcast_to(meta_tile[None], (b,) + meta_tile.shape)
        return _pad_rows(jnp.concatenate([mt, frames], axis=1), total, axis=1)

    sx_p = -(-(KEY_TILE + S) // SCORE_ROWS) * SCORE_ROWS
    o_attn_p = _dsa(
        f_q, f_qi, f_kiwi,
        key_rows(m_kbf, f_kbf.reshape(B, S, HEAD_DIM), sx_p),
        key_rows(m_kibf, f_kibf.reshape(B, S, HEAD_DIM_I), sx_p),
        key_rows(m_vt, f_vt.reshape(B, S // KEY_TILE, HEAD_DIM, KEY_TILE), sx_p // KEY_TILE),
        n_batch=B, n_qblk=S // LANES, causal=True, n_sel=min(TOPK_MAX, S // 4), n_keys=S)

    n_keys_s = P + Ts
    sx_s = -(-(KEY_TILE + P + LANES) // SCORE_ROWS) * SCORE_ROWS
    meta_rows = lambda c: _pad_rows(c[l][:, :N_META], KEY_TILE, axis=1)
    past_vt = jnp.swapaxes(cache_v[l][:, N_META:].reshape(Bs, P // KEY_TILE, KEY_TILE, HEAD_DIM), -1, -2)
    meta_vt = jnp.swapaxes(meta_rows(cache_v), -1, -2)[:, None]
    kbf_s = _pad_rows(jnp.concatenate([meta_rows(cache_k).astype(BF16), cache_k[l][:, N_META:].astype(BF16),
                                       s_kbf.reshape(Bs, LANES, HEAD_DIM)], axis=1), sx_s)
    kibf_s = _pad_rows(jnp.concatenate([meta_rows(cache_idx_k).astype(BF16), cache_idx_k[l][:, N_META:].astype(BF16),
                                        s_kibf.reshape(Bs, LANES, HEAD_DIM_I)], axis=1), sx_s)
    vt_s = _pad_rows(jnp.concatenate([meta_vt.astype(BF16), past_vt.astype(BF16),
                                      s_vt.reshape(Bs, 1, HEAD_DIM, KEY_TILE)], axis=1), sx_s // KEY_TILE)
    o_attn_s = _dsa(s_q, s_qi, s_kiwi, kbf_s, kibf_s, vt_s,
                    n_batch=Bs, n_qblk=1, causal=False, n_sel=min(TOPK_MAX, n_keys_s // 4), n_keys=n_keys_s)
    o_attn_s = o_attn_s.reshape(Bs, LANES, D_MODEL)[:, :Ts].reshape(Bs * Ts, D_MODEL)

    zero_shift = jnp.zeros((1, 1, SHIFT_COLS), F32)
    zero_state = jnp.zeros((1, RW_PAIRS, LANES, LANES), F32)
    _, h_meta = _rwkv(m_rw[None], zero_shift, zero_state, w, n_chunks=1, chunk=N_META)
    shift_meta = m_rw[N_META - 1:N_META].reshape(1, 1, SHIFT_COLS)
    o_rw_p, h_p = _rwkv(f_rw.reshape(B, S, SHIFT_COLS), jnp.broadcast_to(shift_meta, (B, 1, SHIFT_COLS)),
                        jnp.broadcast_to(h_meta, (B, RW_PAIRS, LANES, LANES)), w, n_chunks=S // CHUNK, chunk=CHUNK)
    o_rw_s, h_s = _rwkv(s_rw.reshape(Bs, LANES, SHIFT_COLS), state_shift[l].reshape(Bs, 1, SHIFT_COLS),
                        _state_to_pairs(state_wkv[l]), w, n_chunks=1, chunk=CHUNK)

    y_p = _ffn(_merge(xf, o_attn_p, o_rw_p.reshape(B * S, D_MODEL), w, tm), w, tm)
    xs_flat = x_sample.reshape(Bs * Ts, D_MODEL)
    tm_s = min(tm, Bs * Ts)
    y_s = _ffn(_merge(xs_flat, o_attn_s, o_rw_s.reshape(Bs * Ts, D_MODEL), w, tm_s), w, tm_s)

    def with_meta(meta_rows_, frames, width):
        mt = jnp.broadcast_to(meta_rows_[None, :N_META, :width], (B, N_META, width))
        return jnp.concatenate([mt, frames.reshape(B, S, -1)[:, :, :width]], axis=1)[None]

    take = lambda a, width: a.reshape(Bs, LANES, -1)[:, :Ts, :width][None]
    return (
        y_p.reshape(B, S, D_MODEL),
        y_s.reshape(Bs, Ts, D_MODEL),
        with_meta(m_k, f_k, HEAD_DIM),
        with_meta(m_v, f_v, HEAD_DIM),
        with_meta(m_kiwi, f_kiwi, HEAD_DIM_I),
        _pairs_to_state(h_p)[None],
        f_rw.reshape(B, S, SHIFT_COLS)[:, -1][None],
        take(s_k, HEAD_DIM),
        take(s_v, HEAD_DIM),
        take(s_kiwi, HEAD_DIM_I),
        _pairs_to_state(h_s)[None],
        s_rw.reshape(Bs, LANES, SHIFT_COLS)[:, Ts - 1][None],
    )
```

```python
import functools

import numpy as np
import jax
import jax.numpy as jnp
from jax import lax
from jax.experimental import pallas as pl
from jax.experimental.pallas import tpu as pltpu

F32 = jnp.float32
BF16 = jnp.bfloat16
I32 = jnp.int32

D_MODEL = 1024
N_META = 16
CHUNK = 64
N_HEADS_A = 8
HEAD_DIM = 128
ROPE_DIM_A = HEAD_DIM // 4
N_HEADS_I = 8
HEAD_DIM_I = 64
ROPE_DIM_I = HEAD_DIM_I // 4
TOPK_MAX = 256
ROPE_THETA = 500000.0
RW_HEAD = 64
RW_HEADS = 16
RW_PAIRS = RW_HEADS // 2
LORA_W = 64
LORA_A = 64
LORA_G = 128
RW_GN_EPS = 64e-5
D_FF = 2816
LN_EPS = 1e-5
ALPHA = 2.0 ** 0.25
NEG = -1e30
LOG2_E = 1.4426950408889634
SHIFT_COLS = 3 * D_MODEL + LORA_W + LORA_A + LORA_G
_Q0, _K0, _V0, _QI0, _KI0, _WI0, _G0, _RW0 = 0, 1024, 1152, 1280, 1792, 1856, 1864, 3912
_IN_COLS = _RW0 + SHIFT_COLS

LANES = 128
KEY_TILE = 128
SCORE_ROWS = 512
ATT_ROWS = 256
RW_STREAMS = 2
INT_MIN = -2 ** 31
VMEM_LIMIT = 56 * 1024 * 1024

_NN = (((1,), (0,)), ((), ()))
_NT = (((1,), (1,)), ((), ()))
_TN = (((0,), (0,)), ((), ()))
_BNN = (((2,), (1,)), ((0,), (0,)))
_BNT = (((2,), (2,)), ((0,), (0,)))
_BTN = (((1,), (1,)), ((0,), (0,)))


def _mm(a, b, dn=_NN):
    return lax.dot_general(a, b, dn, preferred_element_type=F32)


def _split3(a):
    hi = a.astype(BF16)
    r1 = a - hi.astype(F32)
    mid = r1.astype(BF16)
    lo = (r1 - mid.astype(F32)).astype(BF16)
    return hi, mid, lo


def _dot_hi(a, b, dn=_NN):
    ah, am, _ = _split3(a)
    bh, bm, _ = _split3(b)
    return _mm(ah, bh, dn) + (_mm(ah, bm, dn) + _mm(am, bh, dn))


def _dot_exact_rhs(a, b_bf16, dn=_NN):
    ah = a.astype(BF16)
    am = (a - ah.astype(F32)).astype(BF16)
    return _mm(ah, b_bf16, dn) + _mm(am, b_bf16, dn)


def _layernorm(x, g, b):
    mu = jnp.mean(x, axis=-1, keepdims=True)
    d = x - mu
    var = jnp.mean(d * d, axis=-1, keepdims=True)
    return d * lax.rsqrt(var + LN_EPS) * g + b


def _const_spec(shape):
    nd = len(shape)
    return pl.BlockSpec(shape, lambda *_: (0,) * nd, pipeline_mode=pl.Buffered(1))


def _rope(x, cos, sin, period, half):
    width = x.shape[1]
    reps = width // LANES
    cc = jnp.concatenate([cos] * reps, axis=1) if reps > 1 else cos
    ss = jnp.concatenate([sin] * reps, axis=1) if reps > 1 else sin
    lane = lax.broadcasted_iota(I32, (1, width), 1) % period
    fwd = pltpu.roll(x, width - half, 1)
    bwd = pltpu.roll(x, half, 1)
    return x * cc + jnp.where(lane < half, fwd, bwd) * ss


def _proj_body(x_ref, tab_ref, g0_ref, b0_ref, wa_ref, wi_ref, wr_ref, ikg_ref, ikb_ref,
               q_ref, k_ref, v_ref, kbf_ref, vt_ref, qi_ref, kiwi_ref, kibf_ref, rw_ref):
    tm = x_ref.shape[0]
    hb = _layernorm(x_ref[...], g0_ref[...], b0_ref[...]).astype(BF16)
    tab = tab_ref[...]
    cos_a, sin_a = tab[:, 0:128], tab[:, 128:256]
    cos_i, sin_i = tab[:, 256:384], tab[:, 384:512]

    pa = _mm(hb, wa_ref[...])
    q = _rope(pa[:, :1024], cos_a, sin_a, HEAD_DIM, ROPE_DIM_A // 2)
    q_ref[...] = (q * (HEAD_DIM ** -0.5 * LOG2_E)).astype(BF16)
    k = _rope(pa[:, 1024:1152], cos_a, sin_a, HEAD_DIM, ROPE_DIM_A // 2)
    k_ref[...] = k
    kbf_ref[...] = k.astype(BF16)
    v = pa[:, 1152:1280]
    v_ref[...] = v
    for j in range(tm // KEY_TILE):
        vt_ref[j] = v[j * KEY_TILE:(j + 1) * KEY_TILE, :].T.astype(BF16)

    pi = _mm(hb, wi_ref[...])
    qi_ref[...] = _rope(pi[:, :512], cos_i, sin_i, HEAD_DIM_I, ROPE_DIM_I // 2).astype(BF16)
    kw = pi[:, 512:640]
    lane = lax.broadcasted_iota(I32, (1, LANES), 1)
    is_ki = lane < HEAD_DIM_I
    mu = jnp.sum(jnp.where(is_ki, kw, 0.0), axis=-1, keepdims=True) * (1.0 / HEAD_DIM_I)
    d = jnp.where(is_ki, kw - mu, 0.0)
    var = jnp.sum(d * d, axis=-1, keepdims=True) * (1.0 / HEAD_DIM_I)
    ki = d * lax.rsqrt(var + LN_EPS) * ikg_ref[...] + ikb_ref[...]
    ki = _rope(ki, cos_i, sin_i, HEAD_DIM_I, ROPE_DIM_I // 2)
    wi_scale = (N_HEADS_I ** -0.5) * (HEAD_DIM_I ** -0.5)
    kiwi_ref[...] = jnp.where(is_ki, ki, jnp.where(lane < HEAD_DIM_I + N_HEADS_I, kw * wi_scale, 0.0))
    kibf_ref[...] = ki[:, :HEAD_DIM_I].astype(BF16)

    rw_ref[...] = _mm(hb, wr_ref[...])


def _proj(x, tab, tab_blocks, tm, w):
    rows = x.shape[0]
    n = rows // tm
    row_spec = lambda width: pl.BlockSpec((tm, width), lambda i: (i, 0))
    out_shape = [
        jax.ShapeDtypeStruct((rows, 1024), BF16),
        jax.ShapeDtypeStruct((rows, HEAD_DIM), F32),
        jax.ShapeDtypeStruct((rows, HEAD_DIM), F32),
        jax.ShapeDtypeStruct((rows, HEAD_DIM), BF16),
        jax.ShapeDtypeStruct((rows // KEY_TILE, HEAD_DIM, KEY_TILE), BF16),
        jax.ShapeDtypeStruct((rows, 512), BF16),
        jax.ShapeDtypeStruct((rows, LANES), F32),
        jax.ShapeDtypeStruct((rows, HEAD_DIM_I), BF16),
        jax.ShapeDtypeStruct((rows, SHIFT_COLS), F32),
    ]
    out_specs = [row_spec(1024), row_spec(HEAD_DIM), row_spec(HEAD_DIM), row_spec(HEAD_DIM),
                 pl.BlockSpec((tm // KEY_TILE, HEAD_DIM, KEY_TILE), lambda i: (i, 0, 0)),
                 row_spec(512), row_spec(LANES), row_spec(HEAD_DIM_I), row_spec(SHIFT_COLS)]
    in_specs = [row_spec(D_MODEL),
                pl.BlockSpec((tm, 512), lambda i: (i % tab_blocks, 0)),
                _const_spec((1, D_MODEL)), _const_spec((1, D_MODEL)),
                _const_spec((D_MODEL, 1280)), _const_spec((D_MODEL, 640)), _const_spec((D_MODEL, SHIFT_COLS)),
                _const_spec((1, LANES)), _const_spec((1, LANES))]
    return pl.pallas_call(
        _proj_body, grid=(n,), in_specs=in_specs, out_specs=out_specs, out_shape=out_shape,
        compiler_params=pltpu.CompilerParams(dimension_semantics=("arbitrary",), vmem_limit_bytes=VMEM_LIMIT),
        name="proj",
    )(x, tab, w["ln0_g"], w["ln0_b"], w["wa"], w["wi"], w["wr"], w["ikg"], w["ikb"])


def _dsa_body(q_ref, qi_ref, kiwi_ref, kbf_ref, kibf_ref, vt_ref, o_ref,
              sc_ref, floor_ref, bias_ref, m_ref, l_ref, acc_ref, s_ref, *, causal, n_sel, n_keys):
    i = pl.program_id(1)
    if causal:
        lim_lo = KEY_TILE + i * (2 * CHUNK) + CHUNK
        lim_hi = lim_lo + CHUNK
    else:
        lim_lo = KEY_TILE + n_keys
        lim_hi = lim_lo
    n_score = (lim_hi + SCORE_ROWS - 1) // SCORE_ROWS

    heads_on_lanes = lambda t, n, rows: jnp.concatenate([t[h * rows:(h + 1) * rows, :] for h in range(n)], axis=1)
    qi_all = heads_on_lanes(qi_ref[...].astype(F32).T.astype(BF16), N_HEADS_I, HEAD_DIM_I)
    w_all = heads_on_lanes(kiwi_ref[...].T[HEAD_DIM_I:HEAD_DIM_I + N_HEADS_I, :], N_HEADS_I, 1)
    lane = lax.broadcasted_iota(I32, (1, LANES), 1)

    def as_score(k):
        return lax.bitcast_convert_type(k ^ ((k >> 31) & 0x7FFFFFFF), F32)

    def score_block(j, carry):
        r0 = pl.multiple_of(j * SCORE_ROWS, SCORE_ROWS)
        s = jnp.maximum(_mm(kibf_ref[pl.ds(r0, SCORE_ROWS), :], qi_all), 0.0) * w_all
        sc = s[:, :LANES]
        for h in range(1, N_HEADS_I):
            sc = sc + s[:, h * LANES:(h + 1) * LANES]
        row = r0 + lax.broadcasted_iota(I32, (SCORE_ROWS, 1), 0)
        adm = (row < N_META) | ((row >= KEY_TILE) & (row < lim_lo)) | ((row >= lim_lo) & (row < lim_hi) & (lane >= CHUNK))
        sc = jnp.where(adm, sc, NEG)
        sc_ref[pl.ds(r0, SCORE_ROWS), :] = sc
        near = sc.astype(BF16).astype(F32)
        nb = lax.bitcast_convert_type(near, I32)
        below = as_score((nb ^ ((nb >> 31) & 0x7FFFFFFF)) - 0x10000)
        floor_ref[pl.ds(r0, SCORE_ROWS), :] = jnp.where(near > sc, below, near).astype(BF16)
        return carry

    lax.fori_loop(0, n_score, score_block, 0)
    tiles_per_blk = SCORE_ROWS // KEY_TILE
    n_tiles = n_score * tiles_per_blk

    def count(pred):
        def body(j, acc):
            for u in range(tiles_per_blk):
                r0 = pl.multiple_of(j * SCORE_ROWS + u * KEY_TILE, KEY_TILE)
                m = pred(sc_ref[pl.ds(r0, KEY_TILE), :], r0)
                acc = jnp.where(m.reshape(KEY_TILE // 8, 8, LANES), acc + 1, acc)
            return acc
        acc = lax.fori_loop(0, n_score, body, jnp.zeros((KEY_TILE // 8, 8, LANES), I32))
        return acc.sum(axis=0).sum(axis=0, keepdims=True)

    def count_floor_ge(cand16):
        def body(j, acc):
            for u in range(tiles_per_blk):
                r0 = pl.multiple_of(j * SCORE_ROWS + u * KEY_TILE, KEY_TILE)
                m = floor_ref[pl.ds(r0, KEY_TILE), :] >= cand16
                acc = jnp.where(m.reshape(KEY_TILE // 16, 16, LANES), acc + 1, acc)
            return acc
        acc = lax.fori_loop(0, n_score, body, jnp.zeros((KEY_TILE // 16, 16, LANES), jnp.int16))
        return acc.astype(I32).sum(axis=0).sum(axis=0, keepdims=True)

    def bit_step(b, st, coarse):
        t, cnt_t = st
        cand = t + lax.shift_left(jnp.int32(1), 31 - b)
        cand_f = as_score(cand)
        cnt = count_floor_ge(cand_f.astype(BF16)) if coarse else count(lambda tile, r0: tile >= cand_f)
        take = cnt >= n_sel
        return jnp.where(take, cand, t), jnp.where(take, cnt, cnt_t)

    all_rows = jnp.zeros((1, LANES), I32) + n_score * SCORE_ROWS
    st = lax.fori_loop(0, 16, functools.partial(bit_step, coarse=True), (jnp.full((1, LANES), INT_MIN, I32), all_rows))
    thr_i, cnt_thr = lax.fori_loop(16, 32, functools.partial(bit_step, coarse=False), st)
    thr = as_score(thr_i)
    big = jnp.full((1, LANES), 2 ** 30, I32)

    def tie_path():
        cnt_gt = count(lambda tile, r0: tile > thr)
        need = n_sel - cnt_gt
        tied = cnt_thr - cnt_gt > need

        def tie_search():
            def rbit(b, r):
                cand = r + lax.shift_left(jnp.int32(1), 14 - b)
                cnt = count(lambda tile, r0: (tile == thr) & ((r0 + lax.broadcasted_iota(I32, (KEY_TILE, 1), 0)) < cand))
                return jnp.where(cnt <= need, cand, r)
            r = lax.fori_loop(0, 15, rbit, jnp.zeros((1, LANES), I32))
            return jnp.where(tied, r, big)

        return lax.cond(jnp.max(tied.astype(I32)) > 0, tie_search, lambda: big)

    row_lim = lax.cond(jnp.max(jnp.abs(cnt_thr - n_sel)) > 0, tie_path, lambda: big)

    def bias_block(j, carry):
        r0 = pl.multiple_of(j * KEY_TILE, KEY_TILE)
        tile = sc_ref[pl.ds(r0, KEY_TILE), :]
        row = r0 + lax.broadcasted_iota(I32, (KEY_TILE, 1), 0)
        sel = ((tile > thr) | ((tile == thr) & (row < row_lim))) & (tile > 0.5 * NEG)
        bias_ref[pl.ds(r0, KEY_TILE), :] = jnp.where(sel, 0.0, NEG)
        return carry

    lax.fori_loop(0, n_tiles, bias_block, 0)

    q_all = heads_on_lanes(q_ref[...].astype(F32).T.astype(BF16), N_HEADS_A, HEAD_DIM)
    m_ref[...] = jnp.full(m_ref.shape, NEG, F32)
    l_ref[...] = jnp.zeros(l_ref.shape, F32)
    acc_ref[...] = jnp.zeros(acc_ref.shape, F32)
    tiles_per_step = ATT_ROWS // KEY_TILE
    ones_rows = jnp.ones((16, ATT_ROWS), BF16)
    pair_w = 2 * LANES

    n_att = n_score * (SCORE_ROWS // ATT_ROWS)

    def logits(j, slot):
        r0 = pl.multiple_of(j * ATT_ROWS, ATT_ROWS)
        s_ref[slot] = _mm(kbf_ref[pl.ds(r0, ATT_ROWS), :], q_all)

    def att_block(j, slot):
        logits(jnp.minimum(j + 1, n_att - 1), 1 - slot)
        r0 = pl.multiple_of(j * ATT_ROWS, ATT_ROWS)
        vtb = jnp.concatenate([vt_ref[j * tiles_per_step + u] for u in range(tiles_per_step)], axis=1)
        vtb = jnp.concatenate([vtb, ones_rows], axis=0)
        bias = bias_ref[pl.ds(r0, ATT_ROWS), :]
        s = s_ref[slot] + jnp.concatenate([bias] * N_HEADS_A, axis=1)
        m_old = m_ref[...]
        m_new = jnp.maximum(m_old, jnp.max(s, axis=0, keepdims=True))
        alpha = jnp.exp2(m_old - m_new)
        p = jnp.exp2(s - m_new).astype(BF16)
        m_ref[...] = m_new
        for pr in range(N_HEADS_A // 2):
            cols = slice(pr * pair_w, (pr + 1) * pair_w)
            pv = _mm(vtb, p[:, cols])
            l_ref[:, cols] = alpha[:, cols] * l_ref[:, cols] + pv[HEAD_DIM:HEAD_DIM + 1, :]
            acc_ref[:, cols] = alpha[:, cols] * acc_ref[:, cols] + pv[:HEAD_DIM, :]

    def att_two(jj, carry):
        att_block(2 * jj, 0)
        att_block(2 * jj + 1, 1)
        return carry

    logits(0, 0)
    lax.fori_loop(0, n_att // 2, att_two, 0)
    out_t = acc_ref[...] * (1.0 / l_ref[...])
    o_ref[...] = jnp.concatenate([out_t[:, h * LANES:(h + 1) * LANES].T for h in range(N_HEADS_A)], axis=1).astype(BF16)


def _dsa(q, qi, kiwi, kbf, kibf, vt, *, n_batch, n_qblk, causal, n_sel, n_keys):
    sx = kbf.shape[1]
    qmap = lambda b, i: (b * n_qblk + i, 0)
    body = functools.partial(_dsa_body, causal=causal, n_sel=n_sel, n_keys=n_keys)
    return pl.pallas_call(
        body, grid=(n_batch, n_qblk),
        in_specs=[pl.BlockSpec((LANES, 1024), qmap), pl.BlockSpec((LANES, 512), qmap), pl.BlockSpec((LANES, LANES), qmap),
                  pl.BlockSpec((None, sx, HEAD_DIM), lambda b, i: (b, 0, 0)),
                  pl.BlockSpec((None, sx, HEAD_DIM_I), lambda b, i: (b, 0, 0)),
                  pl.BlockSpec((None, sx // KEY_TILE, HEAD_DIM, KEY_TILE), lambda b, i: (b, 0, 0, 0))],
        out_specs=pl.BlockSpec((LANES, 1024), qmap),
        out_shape=jax.ShapeDtypeStruct((n_batch * n_qblk * LANES, 1024), BF16),
        scratch_shapes=[pltpu.VMEM((sx, LANES), F32), pltpu.VMEM((sx, LANES), BF16), pltpu.VMEM((sx, LANES), F32),
                        pltpu.VMEM((1, N_HEADS_A * LANES), F32), pltpu.VMEM((1, N_HEADS_A * LANES), F32),
                        pltpu.VMEM((HEAD_DIM, N_HEADS_A * LANES), F32),
                        pltpu.VMEM((2, ATT_ROWS, N_HEADS_A * LANES), F32)],
        compiler_params=pltpu.CompilerParams(dimension_semantics=("arbitrary", "arbitrary"), vmem_limit_bytes=VMEM_LIMIT),
        name="dsa",
    )(q, qi, kiwi, kbf, kibf, vt)


def _rwkv_body(rw_ref, shift0_ref, h0_ref, mu_ref, w0_ref, w2h_ref, w2m_ref, a0_ref, a2_ref, g2_ref,
               kk_ref, ka_ref, rk_ref, gng_ref, gnb_ref,
               out_ref, hfin_ref,
               prev_ref, st_ref, *, chunk, streams):
    c = pl.program_id(1)
    n_c = pl.num_programs(1)
    C = chunk
    P = streams * RW_PAIRS

    @pl.when(c == 0)
    def _():
        prev_ref[...] = shift0_ref[...]
        st_ref[...] = h0_ref[...].reshape(P, LANES, LANES)

    x = rw_ref[...].reshape(streams * C, SHIFT_COLS)
    rowi = lax.broadcasted_iota(I32, (streams * C, 1), 0)
    last = jnp.concatenate([jnp.broadcast_to(prev_ref[s], (C, SHIFT_COLS)) for s in range(streams)], axis=0)
    prev = jnp.where(rowi % C == 0, last, pltpu.roll(x, 1, 0))
    for s in range(streams):
        prev_ref[s] = x[(s + 1) * C - 1:(s + 1) * C, :]
    xs = x + (prev - x) * mu_ref[...]
    r = xs[:, 0:1024]
    k = xs[:, 1024:2048]
    v = xs[:, 2048:3072]
    lw = xs[:, 3072:3072 + LORA_W]
    la = xs[:, 3072 + LORA_W:3072 + LORA_W + LORA_A]
    lg = xs[:, 3072 + LORA_W + LORA_A:SHIFT_COLS]
    tw = jnp.tanh(lw)
    twh = tw.astype(BF16)
    twm = (tw - twh.astype(F32)).astype(BF16)
    wx = w0_ref[...] + (_mm(twh, w2h_ref[...]) + (_mm(twh, w2m_ref[...]) + _mm(twm, w2h_ref[...])))
    w_log = -jax.nn.softplus(-wx) - 0.5
    ld = -jnp.exp(w_log)
    a = jax.nn.sigmoid(a0_ref[...] + _mm(la.astype(BF16), a2_ref[...]))
    g = _mm(jax.nn.sigmoid(lg).astype(BF16), g2_ref[...])
    tr = lax.broadcasted_iota(I32, (streams * C, streams * C), 0)
    tc = lax.broadcasted_iota(I32, (streams * C, streams * C), 1)
    tri = ((tr >= tc) & (tr // C == tc // C)).astype(BF16)
    cl = _cumsum_rows(ld, tri)

    def pairs(t):
        return jnp.stack([t[s * C:(s + 1) * C, p * LANES:(p + 1) * LANES]
                          for s in range(streams) for p in range(RW_PAIRS)], axis=0)

    rp, kp, vp, ap, ldp, clp, gp = pairs(r), pairs(k), pairs(v), pairs(a), pairs(ld), pairs(cl), pairs(g)
    per_stream = lambda ref: jnp.concatenate([ref[...]] * streams, axis=0)
    kkw, kaw, rkw, gng, gnb = (per_stream(ref) for ref in (kk_ref, ka_ref, rk_ref, gng_ref, gnb_ref))

    lane = lax.broadcasted_iota(I32, (1, 1, LANES), 2)
    lane_h0 = lane < RW_HEAD
    r2 = lax.broadcasted_iota(I32, (2 * C, 2 * C), 0)
    c2 = lax.broadcasted_iota(I32, (2 * C, 2 * C), 1)
    same_head = (r2 < C) == (c2 < C)
    strict = same_head & ((r2 % C) > (c2 % C))
    incl = same_head & ((r2 % C) >= (c2 % C))
    rl = lax.broadcasted_iota(I32, (LANES, LANES), 0)
    cl_ = lax.broadcasted_iota(I32, (LANES, LANES), 1)
    blockdiag = (rl < RW_HEAD) == (cl_ < RW_HEAD)
    eye = rl == cl_
    ones_bd = blockdiag.astype(BF16)

    def stack(t):
        return jnp.concatenate([jnp.where(lane_h0, t, 0.0), jnp.where(lane_h0, 0.0, t)], axis=1)

    def head_sum(t, split=False):
        t2 = t.reshape(P * C, LANES)
        s = _dot_exact_rhs(t2, ones_bd) if split else _mm(t2.astype(BF16), ones_bd)
        return s.reshape(P, C, LANES)

    bf = lambda t: t.astype(BF16)
    kk = kp * kkw
    kk = kk / jnp.maximum(jnp.sqrt(head_sum(kk * kk, split=True)), 1e-12)
    k2 = kp * (1.0 + (ap - 1.0) * kaw)
    e_in = jnp.exp(clp)
    e_ex = jnp.exp(clp - ldp)
    e_neg = jnp.exp(-clp)
    e_c = e_in[:, C - 1:C, :]
    at = -kk * e_ex
    rt = rp * e_in
    bt = kk * ap * e_neg
    kt = k2 * e_neg
    hbd = st_ref[...]

    a_s, r_s, b_s, k_s, v_s = stack(bf(at)), stack(bf(rt)), stack(bf(bt)), stack(bf(kt)), stack(bf(vp))
    ar = jnp.concatenate([a_s, r_s], axis=1)
    if (2 * C) % LANES == 0:
        m = _mm(ar, jnp.concatenate([b_s, k_s], axis=1), _BNT)
        m_ab, m_ak = m[:, :2 * C, :2 * C], m[:, :2 * C, 2 * C:]
        m_rb, m_rk = m[:, 2 * C:, :2 * C], m[:, 2 * C:, 2 * C:]
    else:
        m_ab, m_ak = _mm(a_s, b_s, _BNT), _mm(a_s, k_s, _BNT)
        m_rb, m_rk = _mm(r_s, b_s, _BNT), _mm(r_s, k_s, _BNT)
    xpow = bf(jnp.where(strict, m_ab, 0.0))
    m_ak = bf(jnp.where(strict, m_ak, 0.0))
    m_rb = bf(jnp.where(incl, m_rb, 0.0))
    m_rk = bf(jnp.where(incl, m_rk, 0.0))

    arh = _mm(ar, bf(hbd), _BNN)
    arh_y = arh[:, 2 * C:]
    u_s = arh[:, :2 * C] + _mm(m_ak, v_s, _BNN)
    n = 1
    while n < C:
        u_s = u_s + _mm(xpow, bf(u_s), _BNN)
        n *= 2
        if n < C:
            xpow = bf(_mm(xpow, xpow, _BNN))
    if (2 * C) % LANES == 0:
        y_s = arh_y + _mm(jnp.concatenate([m_rb, m_rk], axis=2), jnp.concatenate([bf(u_s), v_s], axis=1), _BNN)
    else:
        y_s = arh_y + (_mm(m_rb, bf(u_s), _BNN) + _mm(m_rk, v_s, _BNN))
    y = y_s[:, :C] + y_s[:, C:]
    u = u_s[:, :C] + u_s[:, C:]
    dmat = jnp.where(eye, jnp.broadcast_to(e_c, (P, LANES, LANES)), 0.0)
    lhs = jnp.concatenate([bt * e_c, kt * e_c, dmat], axis=1)
    rhs = jnp.concatenate([u, vp, hbd], axis=1)
    st_ref[...] = jnp.where(blockdiag, _mm(bf(lhs), bf(rhs), _BTN), 0.0)

    mu_y = head_sum(y) * (1.0 / RW_HEAD)
    dy = y - mu_y
    var = head_sum(dy * dy) * (1.0 / RW_HEAD)
    yn = dy * lax.rsqrt(var + RW_GN_EPS) * gng + gnb
    bonus = head_sum(rp * k2 * rkw) * vp
    yo = ((yn + bonus) * gp).astype(BF16)
    for s in range(streams):
        out_ref[s] = jnp.concatenate([yo[s * RW_PAIRS + p] for p in range(RW_PAIRS)], axis=1)

    @pl.when(c == n_c - 1)
    def _():
        hfin_ref[...] = st_ref[...].reshape(streams, RW_PAIRS, LANES, LANES)


def _cumsum_rows(x, tri_bf16):
    xh, xm, xl = _split3(x)
    return _mm(tri_bf16, xh) + (_mm(tri_bf16, xm) + _mm(tri_bf16, xl))


def _rwkv(rw, shift0, h0, w, *, n_chunks, chunk):
    n_batch = rw.shape[0]
    streams = RW_STREAMS if n_batch % RW_STREAMS == 0 else 1
    body = functools.partial(_rwkv_body, chunk=chunk, streams=streams)
    pp = lambda: _const_spec((RW_PAIRS, 1, LANES))
    return pl.pallas_call(
        body, grid=(n_batch // streams, n_chunks),
        in_specs=[pl.BlockSpec((streams, chunk, SHIFT_COLS), lambda b, c: (b, c, 0)),
                  pl.BlockSpec((streams, 1, SHIFT_COLS), lambda b, c: (b, 0, 0)),
                  pl.BlockSpec((streams, RW_PAIRS, LANES, LANES), lambda b, c: (b, 0, 0, 0)),
                  _const_spec((1, SHIFT_COLS)), _const_spec((1, D_MODEL)), _const_spec((LORA_W, D_MODEL)),
                  _const_spec((LORA_W, D_MODEL)),
                  _const_spec((1, D_MODEL)), _const_spec((LORA_A, D_MODEL)), _const_spec((LORA_G, D_MODEL)),
                  pp(), pp(), pp(), pp(), pp()],
        out_specs=[pl.BlockSpec((streams, chunk, D_MODEL), lambda b, c: (b, c, 0)),
                   pl.BlockSpec((streams, RW_PAIRS, LANES, LANES), lambda b, c: (b, 0, 0, 0))],
        out_shape=[jax.ShapeDtypeStruct((n_batch, n_chunks * chunk, D_MODEL), BF16),
                   jax.ShapeDtypeStruct((n_batch, RW_PAIRS, LANES, LANES), F32)],
        scratch_shapes=[pltpu.VMEM((streams, 1, SHIFT_COLS), F32), pltpu.VMEM((streams * RW_PAIRS, LANES, LANES), F32)],
        compiler_params=pltpu.CompilerParams(dimension_semantics=("arbitrary", "arbitrary"), vmem_limit_bytes=VMEM_LIMIT),
        name="rwkv",
    )(rw, shift0, h0, w["rw_mu"], w["rw_w0"], w["rw_w2h"], w["rw_w2m"], w["rw_a0"], w["rw_a2"], w["rw_g2"],
      w["rw_k_k"], w["rw_k_a"], w["rw_r_k"], w["rw_gn_g"], w["rw_gn_b"])


def _merge_body(x_ref, oa_ref, orw_ref, g0_ref, b0_ref, wg_ref, woa_ref, wor_ref, wout_ref, g1_ref, b1_ref, x1_ref):
    h = _layernorm(x_ref[...], g0_ref[...], b0_ref[...])
    gates = jax.nn.sigmoid(_mm(h.astype(BF16), wg_ref[...]))
    o_attn = _mm(oa_ref[...], woa_ref[...])
    o_rw = _mm(orw_ref[...], wor_ref[...])
    mixed = _mm((gates[:, :D_MODEL] * o_attn + gates[:, D_MODEL:] * o_rw).astype(BF16), wout_ref[...])
    x1_ref[...] = _layernorm(ALPHA * h + mixed, g1_ref[...], b1_ref[...])


def _merge(x, oa, orw, w, tm):
    rows = x.shape[0]
    spec = lambda: pl.BlockSpec((tm, D_MODEL), lambda i: (i, 0))
    vec = lambda: _const_spec((1, D_MODEL))
    sq = lambda: _const_spec((D_MODEL, D_MODEL))
    return pl.pallas_call(
        _merge_body, grid=(rows // tm,),
        in_specs=[spec(), spec(), spec(), vec(), vec(), _const_spec((D_MODEL, 2 * D_MODEL)), sq(), sq(), sq(), vec(), vec()],
        out_specs=spec(), out_shape=jax.ShapeDtypeStruct((rows, D_MODEL), F32),
        compiler_params=pltpu.CompilerParams(dimension_semantics=("arbitrary",), vmem_limit_bytes=VMEM_LIMIT),
        name="merge",
    )(x, oa, orw, w["ln0_g"], w["ln0_b"], w["wg"], w["w_o_attn"], w["w_o_rwkv"], w["w_out"], w["ln1_g"], w["ln1_b"])


def _ffn_body(x1_ref, wgate_ref, wup_ref, wdown_ref, g2_ref, b2_ref, y_ref):
    x1 = x1_ref[...]
    xb = x1.astype(BF16)
    hidden = jax.nn.silu(_mm(xb, wgate_ref[...])) * _mm(xb, wup_ref[...])
    ffn = _mm(hidden.astype(BF16), wdown_ref[...])
    y_ref[...] = _layernorm(ALPHA * x1 + ffn, g2_ref[...], b2_ref[...])


def _ffn(x1, w, tm):
    rows = x1.shape[0]
    spec = lambda: pl.BlockSpec((tm, D_MODEL), lambda i: (i, 0))
    return pl.pallas_call(
        _ffn_body, grid=(rows // tm,),
        in_specs=[spec(), _const_spec((D_MODEL, D_FF)), _const_spec((D_MODEL, D_FF)), _const_spec((D_FF, D_MODEL)),
                  _const_spec((1, D_MODEL)), _const_spec((1, D_MODEL))],
        out_specs=spec(), out_shape=jax.ShapeDtypeStruct((rows, D_MODEL), F32),
        compiler_params=pltpu.CompilerParams(dimension_semantics=("arbitrary",), vmem_limit_bytes=VMEM_LIMIT),
        name="ffn",
    )(x1, w["ffn_w_gate"], w["ffn_w_up"], w["ffn_w_down"], w["ln2_g"], w["ln2_b"])


def _rope_table(pos):
    pos = pos.astype(F32)[:, None]

    def one(rot, period):
        half = rot // 2
        inv = ROPE_THETA ** (-jnp.arange(0, rot, 2, dtype=F32) / rot)
        ang = pos * inv[None]
        cos, sin = jnp.cos(ang), jnp.sin(ang)
        t = pos.shape[0]
        ones = jnp.ones((t, period - rot), F32)
        zeros = jnp.zeros((t, period - rot), F32)
        c = jnp.concatenate([cos, cos, ones], axis=1)
        s = jnp.concatenate([-sin, sin, zeros], axis=1)
        reps = LANES // period
        return jnp.tile(c, (1, reps)), jnp.tile(s, (1, reps))

    ca, sa = one(ROPE_DIM_A, HEAD_DIM)
    ci, si = one(ROPE_DIM_I, HEAD_DIM_I)
    return jnp.concatenate([ca, sa, ci, si], axis=1)


def _pairs(vec):
    return vec.reshape(RW_PAIRS, 1, LANES).astype(F32)


def _state_to_pairs(s):
    b = s.shape[0]
    ht = jnp.swapaxes(s, -1, -2).reshape(b, RW_PAIRS, 2, RW_HEAD, RW_HEAD)
    z = jnp.zeros((b, RW_PAIRS, RW_HEAD, RW_HEAD), s.dtype)
    top = jnp.concatenate([ht[:, :, 0], z], axis=-1)
    bot = jnp.concatenate([z, ht[:, :, 1]], axis=-1)
    return jnp.concatenate([top, bot], axis=-2)


def _pairs_to_state(h):
    b = h.shape[0]
    h0 = h[:, :, :RW_HEAD, :RW_HEAD]
    h1 = h[:, :, RW_HEAD:, RW_HEAD:]
    ht = jnp.stack([h0, h1], axis=2).reshape(b, RW_HEADS, RW_HEAD, RW_HEAD)
    return jnp.swapaxes(ht, -1, -2)


def _pad_rows(a, rows, axis=1):
    pad = [(0, 0)] * a.ndim
    pad[axis] = (0, rows - a.shape[axis])
    return jnp.pad(a, pad)


def kernel(x_prompt, x_sample, cache_k, cache_v, cache_idx_k, state_wkv, state_shift, meta_tokens, ln0_g, ln0_b, w_in, idx_k_ln_g, idx_k_ln_b, rw_mu, rw_w0, rw_w2, rw_a0, rw_a2, rw_g2, rw_k_k, rw_k_a, rw_r_k, rw_gn_g, rw_gn_b, w_o_attn, w_o_rwkv, w_out, ln1_g, ln1_b, ffn_w_gate, ffn_w_up, ffn_w_down, ln2_g, ln2_b):
    B, S, _ = x_prompt.shape
    Bs, Ts, _ = x_sample.shape
    P = cache_k.shape[2] - N_META
    depth = w_in.shape[0]
    assert depth == 1 and S % (2 * CHUNK) == 0 and P % KEY_TILE == 0 and Ts == CHUNK
    l = 0
    row = lambda v: v.reshape(1, -1).astype(F32)
    win = w_in[l]
    wi_pack = jnp.concatenate([win[:, _QI0:_G0], jnp.zeros((D_MODEL, 640 - (_G0 - _QI0)), F32)], axis=1)
    lane_pad = lambda v: jnp.concatenate([v, jnp.zeros((LANES - HEAD_DIM_I,), F32)]).reshape(1, LANES)
    w = {
        "ln0_g": row(ln0_g), "ln0_b": row(ln0_b),
        "wa": win[:, _Q0:_QI0].astype(BF16), "wi": wi_pack.astype(BF16), "wr": win[:, _RW0:].astype(BF16),
        "wg": win[:, _G0:_RW0].astype(BF16),
        "ikg": lane_pad(idx_k_ln_g[l]), "ikb": lane_pad(idx_k_ln_b[l]),
        "rw_mu": row(rw_mu[l]), "rw_w0": row(rw_w0[l]), "rw_a0": row(rw_a0[l]),
        "rw_w2h": rw_w2[l].astype(BF16), "rw_w2m": (rw_w2[l] - rw_w2[l].astype(BF16).astype(F32)).astype(BF16),
        "rw_a2": rw_a2[l].astype(BF16), "rw_g2": rw_g2[l].astype(BF16),
        "rw_k_k": _pairs(rw_k_k[l]), "rw_k_a": _pairs(rw_k_a[l]), "rw_r_k": _pairs(rw_r_k[l]),
        "rw_gn_g": _pairs(rw_gn_g[l]), "rw_gn_b": _pairs(rw_gn_b[l]),
        "w_o_attn": w_o_attn[l].astype(BF16), "w_o_rwkv": w_o_rwkv[l].astype(BF16), "w_out": w_out[l].astype(BF16),
        "ln1_g": row(ln1_g[l]), "ln1_b": row(ln1_b[l]),
        "ffn_w_gate": ffn_w_gate[l].astype(BF16), "ffn_w_up": ffn_w_up[l].astype(BF16),
        "ffn_w_down": ffn_w_down[l].astype(BF16), "ln2_g": row(ln2_g[l]), "ln2_b": row(ln2_b[l]),
    }

    tm = 512
    assert S % tm == 0
    x_meta = _pad_rows(meta_tokens.astype(F32), KEY_TILE, axis=0)
    m_q, m_k, m_v, m_kbf, m_vt, m_qi, m_kiwi, m_kibf, m_rw = _proj(
        x_meta, _rope_table(jnp.arange(KEY_TILE)), 1, KEY_TILE, w)
    xf = x_prompt.reshape(B * S, D_MODEL)
    f_q, f_k, f_v, f_kbf, f_vt, f_qi, f_kiwi, f_kibf, f_rw = _proj(
        xf, _rope_table(N_META + jnp.arange(S)), S // tm, tm, w)
    xs_pad = _pad_rows(x_sample, LANES, axis=1).reshape(Bs * LANES, D_MODEL)
    s_q, s_k, s_v, s_kbf, s_vt, s_qi, s_kiwi, s_kibf, s_rw = _proj(
        xs_pad, _rope_table(N_META + P + jnp.arange(LANES)), 1, LANES, w)

    def key_rows(meta_tile, frames, total):
        b = frames.shape[0]
        mt = jnp.broadcast_to(meta_tile[None], (b,) + meta_tile.shape)
        return _pad_rows(jnp.concatenate([mt, frames], axis=1), total, axis=1)

    sx_p = -(-(KEY_TILE + S) // SCORE_ROWS) * SCORE_ROWS
    o_attn_p = _dsa(
        f_q, f_qi, f_kiwi,
        key_rows(m_kbf, f_kbf.reshape(B, S, HEAD_DIM), sx_p),
        key_rows(m_kibf, f_kibf.reshape(B, S, HEAD_DIM_I), sx_p),
        key_rows(m_vt, f_vt.reshape(B, S // KEY_TILE, HEAD_DIM, KEY_TILE), sx_p // KEY_TILE),
        n_batch=B, n_qblk=S // LANES, causal=True, n_sel=min(TOPK_MAX, S // 4), n_keys=S)

    n_keys_s = P + Ts
    sx_s = -(-(KEY_TILE + P + LANES) // SCORE_ROWS) * SCORE_ROWS
    meta_rows = lambda c: _pad_rows(c[l][:, :N_META], KEY_TILE, axis=1)
    past_vt = jnp.swapaxes(cache_v[l][:, N_META:].reshape(Bs, P // KEY_TILE, KEY_TILE, HEAD_DIM), -1, -2)
    meta_vt = jnp.swapaxes(meta_rows(cache_v), -1, -2)[:, None]
    kbf_s = _pad_rows(jnp.concatenate([meta_rows(cache_k).astype(BF16), cache_k[l][:, N_META:].astype(BF16),
                                       s_kbf.reshape(Bs, LANES, HEAD_DIM)], axis=1), sx_s)
    kibf_s = _pad_rows(jnp.concatenate([meta_rows(cache_idx_k).astype(BF16), cache_idx_k[l][:, N_META:].astype(BF16),
                                        s_kibf.reshape(Bs, LANES, HEAD_DIM_I)], axis=1), sx_s)
    vt_s = _pad_rows(jnp.concatenate([meta_vt.astype(BF16), past_vt.astype(BF16),
                                      s_vt.reshape(Bs, 1, HEAD_DIM, KEY_TILE)], axis=1), sx_s // KEY_TILE)
    o_attn_s = _dsa(s_q, s_qi, s_kiwi, kbf_s, kibf_s, vt_s,
                    n_batch=Bs, n_qblk=1, causal=False, n_sel=min(TOPK_MAX, n_keys_s // 4), n_keys=n_keys_s)
    o_attn_s = o_attn_s.reshape(Bs, LANES, D_MODEL)[:, :Ts].reshape(Bs * Ts, D_MODEL)

    zero_shift = jnp.zeros((1, 1, SHIFT_COLS), F32)
    zero_state = jnp.zeros((1, RW_PAIRS, LANES, LANES), F32)
    _, h_meta = _rwkv(m_rw[None], zero_shift, zero_state, w, n_chunks=1, chunk=N_META)
    shift_meta = m_rw[N_META - 1:N_META].reshape(1, 1, SHIFT_COLS)
    o_rw_p, h_p = _rwkv(f_rw.reshape(B, S, SHIFT_COLS), jnp.broadcast_to(shift_meta, (B, 1, SHIFT_COLS)),
                        jnp.broadcast_to(h_meta, (B, RW_PAIRS, LANES, LANES)), w, n_chunks=S // CHUNK, chunk=CHUNK)
    o_rw_s, h_s = _rwkv(s_rw.reshape(Bs, LANES, SHIFT_COLS), state_shift[l].reshape(Bs, 1, SHIFT_COLS),
                        _state_to_pairs(state_wkv[l]), w, n_chunks=1, chunk=CHUNK)

    y_p = _ffn(_merge(xf, o_attn_p, o_rw_p.reshape(B * S, D_MODEL), w, tm), w, tm)
    xs_flat = x_sample.reshape(Bs * Ts, D_MODEL)
    tm_s = min(tm, Bs * Ts)
    y_s = _ffn(_merge(xs_flat, o_attn_s, o_rw_s.reshape(Bs * Ts, D_MODEL), w, tm_s), w, tm_s)

    def with_meta(meta_rows_, frames, width):
        mt = jnp.broadcast_to(meta_rows_[None, :N_META, :width], (B, N_META, width))
        return jnp.concatenate([mt, frames.reshape(B, S, -1)[:, :, :width]], axis=1)[None]

    take = lambda a, width: a.reshape(Bs, LANES, -1)[:, :Ts, :width][None]
    return (
        y_p.reshape(B, S, D_MODEL),
        y_s.reshape(Bs, Ts, D_MODEL),
        with_meta(m_k, f_k, HEAD_DIM),
        with_meta(m_v, f_v, HEAD_DIM),
        with_meta(m_kiwi, f_kiwi, HEAD_DIM_I),
        _pairs_to_state(h_p)[None],
        f_rw.reshape(B, S, SHIFT_COLS)[:, -1][None],
        take(s_k, HEAD_DIM),
        take(s_v, HEAD_DIM),
        take(s_kiwi, HEAD_DIM_I),
        _pairs_to_state(h_s)[None],
        s_rw.reshape(Bs, LANES, SHIFT_COLS)[:, Ts - 1][None],
    )
```

```python
import functools

import numpy as np
import jax
import jax.numpy as jnp
from jax import lax
from jax.experimental import pallas as pl
from jax.experimental.pallas import tpu as pltpu

F32 = jnp.float32
BF16 = jnp.bfloat16
I32 = jnp.int32

D_MODEL = 1024
N_META = 16
CHUNK = 64
N_HEADS_A = 8
HEAD_DIM = 128
ROPE_DIM_A = HEAD_DIM // 4
N_HEADS_I = 8
HEAD_DIM_I = 64
ROPE_DIM_I = HEAD_DIM_I // 4
TOPK_MAX = 256
ROPE_THETA = 500000.0
RW_HEAD = 64
RW_HEADS = 16
RW_PAIRS = RW_HEADS // 2
LORA_W = 64
LORA_A = 64
LORA_G = 128
RW_GN_EPS = 64e-5
D_FF = 2816
LN_EPS = 1e-5
ALPHA = 2.0 ** 0.25
NEG = -1e30
LOG2_E = 1.4426950408889634
SHIFT_COLS = 3 * D_MODEL + LORA_W + LORA_A + LORA_G
_Q0, _K0, _V0, _QI0, _KI0, _WI0, _G0, _RW0 = 0, 1024, 1152, 1280, 1792, 1856, 1864, 3912
_IN_COLS = _RW0 + SHIFT_COLS

LANES = 128
KEY_TILE = 128
SCORE_ROWS = 512
ATT_ROWS = 256
RW_STREAMS = 2
INT_MIN = -2 ** 31
VMEM_LIMIT = 56 * 1024 * 1024

_NN = (((1,), (0,)), ((), ()))
_NT = (((1,), (1,)), ((), ()))
_TN = (((0,), (0,)), ((), ()))
_BNN = (((2,), (1,)), ((0,), (0,)))
_BNT = (((2,), (2,)), ((0,), (0,)))
_BTN = (((1,), (1,)), ((0,), (0,)))


def _mm(a, b, dn=_NN):
    return lax.dot_general(a, b, dn, preferred_element_type=F32)


def _split3(a):
    hi = a.astype(BF16)
    r1 = a - hi.astype(F32)
    mid = r1.astype(BF16)
    lo = (r1 - mid.astype(F32)).astype(BF16)
    return hi, mid, lo


def _dot_hi(a, b, dn=_NN):
    ah, am, _ = _split3(a)
    bh, bm, _ = _split3(b)
    return _mm(ah, bh, dn) + (_mm(ah, bm, dn) + _mm(am, bh, dn))


def _dot_exact_rhs(a, b_bf16, dn=_NN):
    ah = a.astype(BF16)
    am = (a - ah.astype(F32)).astype(BF16)
    return _mm(ah, b_bf16, dn) + _mm(am, b_bf16, dn)


def _layernorm(x, g, b):
    mu = jnp.mean(x, axis=-1, keepdims=True)
    d = x - mu
    var = jnp.mean(d * d, axis=-1, keepdims=True)
    return d * lax.rsqrt(var + LN_EPS) * g + b


def _const_spec(shape):
    nd = len(shape)
    return pl.BlockSpec(shape, lambda *_: (0,) * nd, pipeline_mode=pl.Buffered(1))


def _rope(x, cos, sin, period, half):
    width = x.shape[1]
    reps = width // LANES
    cc = jnp.concatenate([cos] * reps, axis=1) if reps > 1 else cos
    ss = jnp.concatenate([sin] * reps, axis=1) if reps > 1 else sin
    lane = lax.broadcasted_iota(I32, (1, width), 1) % period
    fwd = pltpu.roll(x, width - half, 1)
    bwd = pltpu.roll(x, half, 1)
    return x * cc + jnp.where(lane < half, fwd, bwd) * ss


def _heads_on_lanes(t, n, rows):
    return jnp.concatenate([t[h * rows:(h + 1) * rows, :] for h in range(n)], axis=1)


def _proj_body(x_ref, tab_ref, g0_ref, b0_ref, wa_ref, wi_ref, wr_ref, ikg_ref, ikb_ref,
               q_ref, k_ref, v_ref, kbf_ref, vt_ref, qi_ref, kiwi_ref, kibf_ref, rw_ref, wall_ref):
    tm = x_ref.shape[0]
    tiles = [slice(j * KEY_TILE, (j + 1) * KEY_TILE) for j in range(tm // KEY_TILE)]
    hb = _layernorm(x_ref[...], g0_ref[...], b0_ref[...]).astype(BF16)
    tab = tab_ref[...]
    cos_a, sin_a = tab[:, 0:128], tab[:, 128:256]
    cos_i, sin_i = tab[:, 256:384], tab[:, 384:512]

    pa = _mm(hb, wa_ref[...])
    q = _rope(pa[:, :1024], cos_a, sin_a, HEAD_DIM, ROPE_DIM_A // 2)
    q = q * (HEAD_DIM ** -0.5 * LOG2_E)
    for j, rows in enumerate(tiles):
        q_ref[j] = _heads_on_lanes(q[rows].T.astype(BF16), N_HEADS_A, HEAD_DIM)
    k = _rope(pa[:, 1024:1152], cos_a, sin_a, HEAD_DIM, ROPE_DIM_A // 2)
    k_ref[...] = k
    kbf_ref[...] = k.astype(BF16)
    v = pa[:, 1152:1280]
    v_ref[...] = v
    for j, rows in enumerate(tiles):
        vt_ref[j] = v[rows].T.astype(BF16)

    pi = _mm(hb, wi_ref[...])
    qi = _rope(pi[:, :512], cos_i, sin_i, HEAD_DIM_I, ROPE_DIM_I // 2)
    for j, rows in enumerate(tiles):
        qi_ref[j] = _heads_on_lanes(qi[rows].T.astype(BF16), N_HEADS_I, HEAD_DIM_I)
    kw = pi[:, 512:640]
    lane = lax.broadcasted_iota(I32, (1, LANES), 1)
    is_ki = lane < HEAD_DIM_I
    mu = jnp.sum(jnp.where(is_ki, kw, 0.0), axis=-1, keepdims=True) * (1.0 / HEAD_DIM_I)
    d = jnp.where(is_ki, kw - mu, 0.0)
    var = jnp.sum(d * d, axis=-1, keepdims=True) * (1.0 / HEAD_DIM_I)
    ki = d * lax.rsqrt(var + LN_EPS) * ikg_ref[...] + ikb_ref[...]
    ki = _rope(ki, cos_i, sin_i, HEAD_DIM_I, ROPE_DIM_I // 2)
    wi_scale = (N_HEADS_I ** -0.5) * (HEAD_DIM_I ** -0.5)
    kiwi = jnp.where(is_ki, ki, jnp.where(lane < HEAD_DIM_I + N_HEADS_I, kw * wi_scale, 0.0))
    kiwi_ref[...] = kiwi
    kibf_ref[...] = ki[:, :HEAD_DIM_I].astype(BF16)
    for j, rows in enumerate(tiles):
        wall_ref[j] = _heads_on_lanes(kiwi[rows].T[HEAD_DIM_I:HEAD_DIM_I + N_HEADS_I, :], N_HEADS_I, 1)

    rw_ref[...] = _mm(hb, wr_ref[...])


def _proj(x, tab, tab_blocks, tm, w):
    rows = x.shape[0]
    n = rows // tm
    row_spec = lambda width: pl.BlockSpec((tm, width), lambda i: (i, 0))
    out_shape = [
        jax.ShapeDtypeStruct((rows // KEY_TILE, HEAD_DIM, N_HEADS_A * LANES), BF16),
        jax.ShapeDtypeStruct((rows, HEAD_DIM), F32),
        jax.ShapeDtypeStruct((rows, HEAD_DIM), F32),
        jax.ShapeDtypeStruct((rows, HEAD_DIM), BF16),
        jax.ShapeDtypeStruct((rows // KEY_TILE, HEAD_DIM, KEY_TILE), BF16),
        jax.ShapeDtypeStruct((rows // KEY_TILE, HEAD_DIM_I, N_HEADS_I * LANES), BF16),
        jax.ShapeDtypeStruct((rows, LANES), F32),
        jax.ShapeDtypeStruct((rows, HEAD_DIM_I), BF16),
        jax.ShapeDtypeStruct((rows, SHIFT_COLS), F32),
        jax.ShapeDtypeStruct((rows // KEY_TILE, 1, N_HEADS_I * LANES), F32),
    ]
    tile_spec = lambda r, width: pl.BlockSpec((tm // KEY_TILE, r, width), lambda i: (i, 0, 0))
    out_specs = [tile_spec(HEAD_DIM, N_HEADS_A * LANES), row_spec(HEAD_DIM), row_spec(HEAD_DIM), row_spec(HEAD_DIM),
                 tile_spec(HEAD_DIM, KEY_TILE),
                 tile_spec(HEAD_DIM_I, N_HEADS_I * LANES), row_spec(LANES), row_spec(HEAD_DIM_I), row_spec(SHIFT_COLS),
                 tile_spec(1, N_HEADS_I * LANES)]
    in_specs = [row_spec(D_MODEL),
                pl.BlockSpec((tm, 512), lambda i: (i % tab_blocks, 0)),
                _const_spec((1, D_MODEL)), _const_spec((1, D_MODEL)),
                _const_spec((D_MODEL, 1280)), _const_spec((D_MODEL, 640)), _const_spec((D_MODEL, SHIFT_COLS)),
                _const_spec((1, LANES)), _const_spec((1, LANES))]
    return pl.pallas_call(
        _proj_body, grid=(n,), in_specs=in_specs, out_specs=out_specs, out_shape=out_shape,
        compiler_params=pltpu.CompilerParams(dimension_semantics=("arbitrary",), vmem_limit_bytes=VMEM_LIMIT),
        name="proj",
    )(x, tab, w["ln0_g"], w["ln0_b"], w["wa"], w["wi"], w["wr"], w["ikg"], w["ikb"])


def _dsa_body(q_ref, qi_ref, w_ref, kbf_ref, kibf_ref, vt_ref, o_ref,
              sc_ref, bias_ref, m_ref, l_ref, acc_ref, s_ref, *, causal, n_sel, n_keys):
    i = pl.program_id(1)
    if causal:
        lim_lo = KEY_TILE + i * (2 * CHUNK) + CHUNK
        lim_hi = lim_lo + CHUNK
    else:
        lim_lo = KEY_TILE + n_keys
        lim_hi = lim_lo
    n_score = (lim_hi + SCORE_ROWS - 1) // SCORE_ROWS

    qi_all = qi_ref[...]
    w_all = w_ref[...]
    lane = lax.broadcasted_iota(I32, (1, LANES), 1)

    def as_score(k):
        return lax.bitcast_convert_type(k ^ ((k >> 31) & 0x7FFFFFFF), F32)

    def score_block(j, carry):
        r0 = pl.multiple_of(j * SCORE_ROWS, SCORE_ROWS)
        s = jnp.maximum(_mm(kibf_ref[pl.ds(r0, SCORE_ROWS), :], qi_all), 0.0) * w_all
        sc = s[:, :LANES]
        for h in range(1, N_HEADS_I):
            sc = sc + s[:, h * LANES:(h + 1) * LANES]
        row = r0 + lax.broadcasted_iota(I32, (SCORE_ROWS, 1), 0)
        adm = (row < N_META) | ((row >= KEY_TILE) & (row < lim_lo)) | ((row >= lim_lo) & (row < lim_hi) & (lane >= CHUNK))
        sc_ref[pl.ds(r0, SCORE_ROWS), :] = jnp.where(adm, sc, NEG)
        return carry

    lax.fori_loop(0, n_score, score_block, 0)
    tiles_per_blk = SCORE_ROWS // KEY_TILE
    n_tiles = n_score * tiles_per_blk

    def count(pred):
        def body(j, acc):
            for u in range(tiles_per_blk):
                r0 = pl.multiple_of(j * SCORE_ROWS + u * KEY_TILE, KEY_TILE)
                m = pred(sc_ref[pl.ds(r0, KEY_TILE), :], r0)
                acc = jnp.where(m.reshape(KEY_TILE // 8, 8, LANES), acc + 1, acc)
            return acc
        acc = lax.fori_loop(0, n_score, body, jnp.zeros((KEY_TILE // 8, 8, LANES), I32))
        return acc.sum(axis=0).sum(axis=0, keepdims=True)

    def bit_step(b, st):
        t, cnt_t = st
        cand = t + lax.shift_left(jnp.int32(1), 31 - b)
        cand_f = as_score(cand)
        cnt = count(lambda tile, r0: tile >= cand_f)
        take = cnt >= n_sel
        return jnp.where(take, cand, t), jnp.where(take, cnt, cnt_t)

    all_rows = jnp.zeros((1, LANES), I32) + n_score * SCORE_ROWS
    thr_i, cnt_thr = lax.fori_loop(0, 32, bit_step, (jnp.full((1, LANES), INT_MIN, I32), all_rows))
    thr = as_score(thr_i)
    big = jnp.full((1, LANES), 2 ** 30, I32)

    def tie_path():
        cnt_gt = count(lambda tile, r0: tile > thr)
        need = n_sel - cnt_gt
        tied = cnt_thr - cnt_gt > need

        def tie_search():
            def rbit(b, r):
                cand = r + lax.shift_left(jnp.int32(1), 14 - b)
                cnt = count(lambda tile, r0: (tile == thr) & ((r0 + lax.broadcasted_iota(I32, (KEY_TILE, 1), 0)) < cand))
                return jnp.where(cnt <= need, cand, r)
            r = lax.fori_loop(0, 15, rbit, jnp.zeros((1, LANES), I32))
            return jnp.where(tied, r, big)

        return lax.cond(jnp.max(tied.astype(I32)) > 0, tie_search, lambda: big)

    row_lim = lax.cond(jnp.max(jnp.abs(cnt_thr - n_sel)) > 0, tie_path, lambda: big)

    def bias_block(j, carry):
        r0 = pl.multiple_of(j * KEY_TILE, KEY_TILE)
        tile = sc_ref[pl.ds(r0, KEY_TILE), :]
        row = r0 + lax.broadcasted_iota(I32, (KEY_TILE, 1), 0)
        sel = ((tile > thr) | ((tile == thr) & (row < row_lim))) & (tile > 0.5 * NEG)
        bias_ref[pl.ds(r0, KEY_TILE), :] = jnp.where(sel, 0.0, NEG)
        return carry

    lax.fori_loop(0, n_tiles, bias_block, 0)

    q_all = q_ref[...]
    m_ref[...] = jnp.full(m_ref.shape, NEG, F32)
    l_ref[...] = jnp.zeros(l_ref.shape, F32)
    acc_ref[...] = jnp.zeros(acc_ref.shape, F32)
    tiles_per_step = ATT_ROWS // KEY_TILE
    ones_rows = jnp.ones((16, ATT_ROWS), BF16)
    pair_w = 2 * LANES

    n_att = n_score * (SCORE_ROWS // ATT_ROWS)

    def logits(j, slot):
        r0 = pl.multiple_of(j * ATT_ROWS, ATT_ROWS)
        s_ref[slot] = _mm(kbf_ref[pl.ds(r0, ATT_ROWS), :], q_all)

    def att_block(j, slot):
        logits(jnp.minimum(j + 1, n_att - 1), 1 - slot)
        r0 = pl.multiple_of(j * ATT_ROWS, ATT_ROWS)
        vtb = jnp.concatenate([vt_ref[j * tiles_per_step + u] for u in range(tiles_per_step)], axis=1)
        vtb = jnp.concatenate([vtb, ones_rows], axis=0)
        bias = bias_ref[pl.ds(r0, ATT_ROWS), :]
        s = s_ref[slot] + jnp.concatenate([bias] * N_HEADS_A, axis=1)
        m_old = m_ref[...]
        m_new = jnp.maximum(m_old, jnp.max(s, axis=0, keepdims=True))
        alpha = jnp.exp2(m_old - m_new)
        p = jnp.exp2(s - m_new).astype(BF16)
        m_ref[...] = m_new
        for pr in range(N_HEADS_A // 2):
            cols = slice(pr * pair_w, (pr + 1) * pair_w)
            pv = _mm(vtb, p[:, cols])
            l_ref[:, cols] = alpha[:, cols] * l_ref[:, cols] + pv[HEAD_DIM:HEAD_DIM + 1, :]
            acc_ref[:, cols] = alpha[:, cols] * acc_ref[:, cols] + pv[:HEAD_DIM, :]

    def att_two(jj, carry):
        att_block(2 * jj, 0)
        att_block(2 * jj + 1, 1)
        return carry

    logits(0, 0)
    lax.fori_loop(0, n_att // 2, att_two, 0)
    out_t = acc_ref[...] * (1.0 / l_ref[...])
    o_ref[...] = jnp.concatenate([out_t[:, h * LANES:(h + 1) * LANES].T for h in range(N_HEADS_A)], axis=1).astype(BF16)


def _dsa(q, qi, wall, kbf, kibf, vt, *, n_batch, n_qblk, causal, n_sel, n_keys):
    sx = kbf.shape[1]
    qmap = lambda b, i: (b * n_qblk + i, 0)
    blk = lambda r: pl.BlockSpec((None, r, N_HEADS_A * LANES), lambda b, i: (b * n_qblk + i, 0, 0))
    body = functools.partial(_dsa_body, causal=causal, n_sel=n_sel, n_keys=n_keys)
    return pl.pallas_call(
        body, grid=(n_batch, n_qblk),
        in_specs=[blk(HEAD_DIM), blk(HEAD_DIM_I), blk(1),
                  pl.BlockSpec((None, sx, HEAD_DIM), lambda b, i: (b, 0, 0)),
                  pl.BlockSpec((None, sx, HEAD_DIM_I), lambda b, i: (b, 0, 0)),
                  pl.BlockSpec((None, sx // KEY_TILE, HEAD_DIM, KEY_TILE), lambda b, i: (b, 0, 0, 0))],
        out_specs=pl.BlockSpec((LANES, 1024), qmap),
        out_shape=jax.ShapeDtypeStruct((n_batch * n_qblk * LANES, 1024), BF16),
        scratch_shapes=[pltpu.VMEM((sx, LANES), F32), pltpu.VMEM((sx, LANES), F32),
                        pltpu.VMEM((1, N_HEADS_A * LANES), F32), pltpu.VMEM((1, N_HEADS_A * LANES), F32),
                        pltpu.VMEM((HEAD_DIM, N_HEADS_A * LANES), F32),
                        pltpu.VMEM((2, ATT_ROWS, N_HEADS_A * LANES), F32)],
        compiler_params=pltpu.CompilerParams(dimension_semantics=("arbitrary", "arbitrary"), vmem_limit_bytes=VMEM_LIMIT),
        name="dsa",
    )(q, qi, wall, kbf, kibf, vt)


def _rwkv_body(rw_ref, shift0_ref, h0_ref, mu_ref, w0_ref, w2h_ref, w2m_ref, a0_ref, a2_ref, g2_ref,
               kk_ref, ka_ref, rk_ref, gng_ref, gnb_ref,
               out_ref, hfin_ref,
               prev_ref, st_ref, *, chunk, streams):
    c = pl.program_id(1)
    n_c = pl.num_programs(1)
    C = chunk
    P = streams * RW_PAIRS

    @pl.when(c == 0)
    def _():
        prev_ref[...] = shift0_ref[...]
        st_ref[...] = h0_ref[...].reshape(P, LANES, LANES)

    x = rw_ref[...].reshape(streams * C, SHIFT_COLS)
    rowi = lax.broadcasted_iota(I32, (streams * C, 1), 0)
    last = jnp.concatenate([jnp.broadcast_to(prev_ref[s], (C, SHIFT_COLS)) for s in range(streams)], axis=0)
    prev = jnp.where(rowi % C == 0, last, pltpu.roll(x, 1, 0))
    for s in range(streams):
        prev_ref[s] = x[(s + 1) * C - 1:(s + 1) * C, :]
    xs = x + (prev - x) * mu_ref[...]
    r = xs[:, 0:1024]
    k = xs[:, 1024:2048]
    v = xs[:, 2048:3072]
    lw = xs[:, 3072:3072 + LORA_W]
    la = xs[:, 3072 + LORA_W:3072 + LORA_W + LORA_A]
    lg = xs[:, 3072 + LORA_W + LORA_A:SHIFT_COLS]
    tw = jnp.tanh(lw)
    twh = tw.astype(BF16)
    twm = (tw - twh.astype(F32)).astype(BF16)
    wx = w0_ref[...] + (_mm(twh, w2h_ref[...]) + (_mm(twh, w2m_ref[...]) + _mm(twm, w2h_ref[...])))
    w_log = -jax.nn.softplus(-wx) - 0.5
    ld = -jnp.exp(w_log)
    a = jax.nn.sigmoid(a0_ref[...] + _mm(la.astype(BF16), a2_ref[...]))
    g = _mm(jax.nn.sigmoid(lg).astype(BF16), g2_ref[...])
    tr = lax.broadcasted_iota(I32, (streams * C, streams * C), 0)
    tc = lax.broadcasted_iota(I32, (streams * C, streams * C), 1)
    tri = ((tr >= tc) & (tr // C == tc // C)).astype(BF16)
    cl = _cumsum_rows(ld, tri)

    def pairs(t):
        return jnp.stack([t[s * C:(s + 1) * C, p * LANES:(p + 1) * LANES]
                          for s in range(streams) for p in range(RW_PAIRS)], axis=0)

    rp, kp, vp, ap, ldp, clp, gp = pairs(r), pairs(k), pairs(v), pairs(a), pairs(ld), pairs(cl), pairs(g)
    per_stream = lambda ref: jnp.concatenate([ref[...]] * streams, axis=0)
    kkw, kaw, rkw, gng, gnb = (per_stream(ref) for ref in (kk_ref, ka_ref, rk_ref, gng_ref, gnb_ref))

    lane = lax.broadcasted_iota(I32, (1, 1, LANES), 2)
    lane_h0 = lane < RW_HEAD
    r2 = lax.broadcasted_iota(I32, (2 * C, 2 * C), 0)
    c2 = lax.broadcasted_iota(I32, (2 * C, 2 * C), 1)
    same_head = (r2 < C) == (c2 < C)
    strict = same_head & ((r2 % C) > (c2 % C))
    incl = same_head & ((r2 % C) >= (c2 % C))
    rl = lax.broadcasted_iota(I32, (LANES, LANES), 0)
    cl_ = lax.broadcasted_iota(I32, (LANES, LANES), 1)
    blockdiag = (rl < RW_HEAD) == (cl_ < RW_HEAD)
    eye = rl == cl_
    ones_bd = blockdiag.astype(BF16)

    def stack(t):
        return jnp.concatenate([jnp.where(lane_h0, t, 0.0), jnp.where(lane_h0, 0.0, t)], axis=1)

    def head_sum(t, split=False):
        t2 = t.reshape(P * C, LANES)
        s = _dot_exact_rhs(t2, ones_bd) if split else _mm(t2.astype(BF16), ones_bd)
        return s.reshape(P, C, LANES)

    bf = lambda t: t.astype(BF16)
    kk = kp * kkw
    kk = kk / jnp.maximum(jnp.sqrt(head_sum(kk * kk, split=True)), 1e-12)
    k2 = kp * (1.0 + (ap - 1.0) * kaw)
    e_in = jnp.exp(clp)
    e_ex = jnp.exp(clp - ldp)
    e_neg = jnp.exp(-clp)
    e_c = e_in[:, C - 1:C, :]
    at = -kk * e_ex
    rt = rp * e_in
    bt = kk * ap * e_neg
    kt = k2 * e_neg
    hbd = st_ref[...]

    a_s, r_s, b_s, k_s, v_s = stack(bf(at)), stack(bf(rt)), stack(bf(bt)), stack(bf(kt)), stack(bf(vp))
    ar = jnp.concatenate([a_s, r_s], axis=1)
    if (2 * C) % LANES == 0:
        m = _mm(ar, jnp.concatenate([b_s, k_s], axis=1), _BNT)
        m_ab, m_ak = m[:, :2 * C, :2 * C], m[:, :2 * C, 2 * C:]
        m_rb, m_rk = m[:, 2 * C:, :2 * C], m[:, 2 * C:, 2 * C:]
    else:
        m_ab, m_ak = _mm(a_s, b_s, _BNT), _mm(a_s, k_s, _BNT)
        m_rb, m_rk = _mm(r_s, b_s, _BNT), _mm(r_s, k_s, _BNT)
    xpow = bf(jnp.where(strict, m_ab, 0.0))
    m_ak = bf(jnp.where(strict, m_ak, 0.0))
    m_rb = bf(jnp.where(incl, m_rb, 0.0))
    m_rk = bf(jnp.where(incl, m_rk, 0.0))

    arh = _mm(ar, bf(hbd), _BNN)
    arh_y = arh[:, 2 * C:]
    u_s = arh[:, :2 * C] + _mm(m_ak, v_s, _BNN)
    n = 1
    while n < C:
        u_s = u_s + _mm(xpow, bf(u_s), _BNN)
        n *= 2
        if n < C:
            xpow = bf(_mm(xpow, xpow, _BNN))
    if (2 * C) % LANES == 0:
        y_s = arh_y + _mm(jnp.concatenate([m_rb, m_rk], axis=2), jnp.concatenate([bf(u_s), v_s], axis=1), _BNN)
    else:
        y_s = arh_y + (_mm(m_rb, bf(u_s), _BNN) + _mm(m_rk, v_s, _BNN))
    y = y_s[:, :C] + y_s[:, C:]
    u = u_s[:, :C] + u_s[:, C:]
    dmat = jnp.where(eye, jnp.broadcast_to(e_c, (P, LANES, LANES)), 0.0)
    lhs = jnp.concatenate([bt * e_c, kt * e_c, dmat], axis=1)
    rhs = jnp.concatenate([u, vp, hbd], axis=1)
    st_ref[...] = jnp.where(blockdiag, _mm(bf(lhs), bf(rhs), _BTN), 0.0)

    mu_y = head_sum(y) * (1.0 / RW_HEAD)
    dy = y - mu_y
    var = head_sum(dy * dy) * (1.0 / RW_HEAD)
    yn = dy * lax.rsqrt(var + RW_GN_EPS) * gng + gnb
    bonus = head_sum(rp * k2 * rkw) * vp
    yo = ((yn + bonus) * gp).astype(BF16)
    for s in range(streams):
        out_ref[s] = jnp.concatenate([yo[s * RW_PAIRS + p] for p in range(RW_PAIRS)], axis=1)

    @pl.when(c == n_c - 1)
    def _():
        hfin_ref[...] = st_ref[...].reshape(streams, RW_PAIRS, LANES, LANES)


def _cumsum_rows(x, tri_bf16):
    xh, xm, xl = _split3(x)
    return _mm(tri_bf16, xh) + (_mm(tri_bf16, xm) + _mm(tri_bf16, xl))


def _rwkv(rw, shift0, h0, w, *, n_chunks, chunk):
    n_batch = rw.shape[0]
    streams = RW_STREAMS if n_batch % RW_STREAMS == 0 else 1
    body = functools.partial(_rwkv_body, chunk=chunk, streams=streams)
    pp = lambda: _const_spec((RW_PAIRS, 1, LANES))
    return pl.pallas_call(
        body, grid=(n_batch // streams, n_chunks),
        in_specs=[pl.BlockSpec((streams, chunk, SHIFT_COLS), lambda b, c: (b, c, 0)),
                  pl.BlockSpec((streams, 1, SHIFT_COLS), lambda b, c: (b, 0, 0)),
                  pl.BlockSpec((streams, RW_PAIRS, LANES, LANES), lambda b, c: (b, 0, 0, 0)),
                  _const_spec((1, SHIFT_COLS)), _const_spec((1, D_MODEL)), _const_spec((LORA_W, D_MODEL)),
                  _const_spec((LORA_W, D_MODEL)),
                  _const_spec((1, D_MODEL)), _const_spec((LORA_A, D_MODEL)), _const_spec((LORA_G, D_MODEL)),
                  pp(), pp(), pp(), pp(), pp()],
        out_specs=[pl.BlockSpec((streams, chunk, D_MODEL), lambda b, c: (b, c, 0)),
                   pl.BlockSpec((streams, RW_PAIRS, LANES, LANES), lambda b, c: (b, 0, 0, 0))],
        out_shape=[jax.ShapeDtypeStruct((n_batch, n_chunks * chunk, D_MODEL), BF16),
                   jax.ShapeDtypeStruct((n_batch, RW_PAIRS, LANES, LANES), F32)],
        scratch_shapes=[pltpu.VMEM((streams, 1, SHIFT_COLS), F32), pltpu.VMEM((streams * RW_PAIRS, LANES, LANES), F32)],
        compiler_params=pltpu.CompilerParams(dimension_semantics=("arbitrary", "arbitrary"), vmem_limit_bytes=VMEM_LIMIT),
        name="rwkv",
    )(rw, shift0, h0, w["rw_mu"], w["rw_w0"], w["rw_w2h"], w["rw_w2m"], w["rw_a0"], w["rw_a2"], w["rw_g2"],
      w["rw_k_k"], w["rw_k_a"], w["rw_r_k"], w["rw_gn_g"], w["rw_gn_b"])


def _merge_body(x_ref, oa_ref, orw_ref, g0_ref, b0_ref, wg_ref, woa_ref, wor_ref, wout_ref, g1_ref, b1_ref, x1_ref):
    h = _layernorm(x_ref[...], g0_ref[...], b0_ref[...])
    gates = jax.nn.sigmoid(_mm(h.astype(BF16), wg_ref[...]))
    o_attn = _mm(oa_ref[...], woa_ref[...])
    o_rw = _mm(orw_ref[...], wor_ref[...])
    mixed = _mm((gates[:, :D_MODEL] * o_attn + gates[:, D_MODEL:] * o_rw).astype(BF16), wout_ref[...])
    x1_ref[...] = _layernorm(ALPHA * h + mixed, g1_ref[...], b1_ref[...])


def _merge(x, oa, orw, w, tm):
    rows = x.shape[0]
    spec = lambda: pl.BlockSpec((tm, D_MODEL), lambda i: (i, 0))
    vec = lambda: _const_spec((1, D_MODEL))
    sq = lambda: _const_spec((D_MODEL, D_MODEL))
    return pl.pallas_call(
        _merge_body, grid=(rows // tm,),
        in_specs=[spec(), spec(), spec(), vec(), vec(), _const_spec((D_MODEL, 2 * D_MODEL)), sq(), sq(), sq(), vec(), vec()],
        out_specs=spec(), out_shape=jax.ShapeDtypeStruct((rows, D_MODEL), F32),
        compiler_params=pltpu.CompilerParams(dimension_semantics=("arbitrary",), vmem_limit_bytes=VMEM_LIMIT),
        name="merge",
    )(x, oa, orw, w["ln0_g"], w["ln0_b"], w["wg"], w["w_o_attn"], w["w_o_rwkv"], w["w_out"], w["ln1_g"], w["ln1_b"])


def _ffn_body(x1_ref, wgate_ref, wup_ref, wdown_ref, g2_ref, b2_ref, y_ref):
    x1 = x1_ref[...]
    xb = x1.astype(BF16)
    hidden = jax.nn.silu(_mm(xb, wgate_ref[...])) * _mm(xb, wup_ref[...])
    ffn = _mm(hidden.astype(BF16), wdown_ref[...])
    y_ref[...] = _layernorm(ALPHA * x1 + ffn, g2_ref[...], b2_ref[...])


def _ffn(x1, w, tm):
    rows = x1.shape[0]
    spec = lambda: pl.BlockSpec((tm, D_MODEL), lambda i: (i, 0))
    return pl.pallas_call(
        _ffn_body, grid=(rows // tm,),
        in_specs=[spec(), _const_spec((D_MODEL, D_FF)), _const_spec((D_MODEL, D_FF)), _const_spec((D_FF, D_MODEL)),
                  _const_spec((1, D_MODEL)), _const_spec((1, D_MODEL))],
        out_specs=spec(), out_shape=jax.ShapeDtypeStruct((rows, D_MODEL), F32),
        compiler_params=pltpu.CompilerParams(dimension_semantics=("arbitrary",), vmem_limit_bytes=VMEM_LIMIT),
        name="ffn",
    )(x1, w["ffn_w_gate"], w["ffn_w_up"], w["ffn_w_down"], w["ln2_g"], w["ln2_b"])


def _rope_table(pos):
    pos = pos.astype(F32)[:, None]

    def one(rot, period):
        half = rot // 2
        inv = ROPE_THETA ** (-jnp.arange(0, rot, 2, dtype=F32) / rot)
        ang = pos * inv[None]
        cos, sin = jnp.cos(ang), jnp.sin(ang)
        t = pos.shape[0]
        ones = jnp.ones((t, period - rot), F32)
        zeros = jnp.zeros((t, period - rot), F32)
        c = jnp.concatenate([cos, cos, ones], axis=1)
        s = jnp.concatenate([-sin, sin, zeros], axis=1)
        reps = LANES // period
        return jnp.tile(c, (1, reps)), jnp.tile(s, (1, reps))

    ca, sa = one(ROPE_DIM_A, HEAD_DIM)
    ci, si = one(ROPE_DIM_I, HEAD_DIM_I)
    return jnp.concatenate([ca, sa, ci, si], axis=1)


def _pairs(vec):
    return vec.reshape(RW_PAIRS, 1, LANES).astype(F32)


def _state_to_pairs(s):
    b = s.shape[0]
    ht = jnp.swapaxes(s, -1, -2).reshape(b, RW_PAIRS, 2, RW_HEAD, RW_HEAD)
    z = jnp.zeros((b, RW_PAIRS, RW_HEAD, RW_HEAD), s.dtype)
    top = jnp.concatenate([ht[:, :, 0], z], axis=-1)
    bot = jnp.concatenate([z, ht[:, :, 1]], axis=-1)
    return jnp.concatenate([top, bot], axis=-2)


def _pairs_to_state(h):
    b = h.shape[0]
    h0 = h[:, :, :RW_HEAD, :RW_HEAD]
    h1 = h[:, :, RW_HEAD:, RW_HEAD:]
    ht = jnp.stack([h0, h1], axis=2).reshape(b, RW_HEADS, RW_HEAD, RW_HEAD)
    return jnp.swapaxes(ht, -1, -2)


def _pad_rows(a, rows, axis=1):
    pad = [(0, 0)] * a.ndim
    pad[axis] = (0, rows - a.shape[axis])
    return jnp.pad(a, pad)


def kernel(x_prompt, x_sample, cache_k, cache_v, cache_idx_k, state_wkv, state_shift, meta_tokens, ln0_g, ln0_b, w_in, idx_k_ln_g, idx_k_ln_b, rw_mu, rw_w0, rw_w2, rw_a0, rw_a2, rw_g2, rw_k_k, rw_k_a, rw_r_k, rw_gn_g, rw_gn_b, w_o_attn, w_o_rwkv, w_out, ln1_g, ln1_b, ffn_w_gate, ffn_w_up, ffn_w_down, ln2_g, ln2_b):
    B, S, _ = x_prompt.shape
    Bs, Ts, _ = x_sample.shape
    P = cache_k.shape[2] - N_META
    depth = w_in.shape[0]
    assert depth == 1 and S % (2 * CHUNK) == 0 and P % KEY_TILE == 0 and Ts == CHUNK
    l = 0
    row = lambda v: v.reshape(1, -1).astype(F32)
    win = w_in[l]
    wi_pack = jnp.concatenate([win[:, _QI0:_G0], jnp.zeros((D_MODEL, 640 - (_G0 - _QI0)), F32)], axis=1)
    lane_pad = lambda v: jnp.concatenate([v, jnp.zeros((LANES - HEAD_DIM_I,), F32)]).reshape(1, LANES)
    w = {
        "ln0_g": row(ln0_g), "ln0_b": row(ln0_b),
        "wa": win[:, _Q0:_QI0].astype(BF16), "wi": wi_pack.astype(BF16), "wr": win[:, _RW0:].astype(BF16),
        "wg": win[:, _G0:_RW0].astype(BF16),
        "ikg": lane_pad(idx_k_ln_g[l]), "ikb": lane_pad(idx_k_ln_b[l]),
        "rw_mu": row(rw_mu[l]), "rw_w0": row(rw_w0[l]), "rw_a0": row(rw_a0[l]),
        "rw_w2h": rw_w2[l].astype(BF16), "rw_w2m": (rw_w2[l] - rw_w2[l].astype(BF16).astype(F32)).astype(BF16),
        "rw_a2": rw_a2[l].astype(BF16), "rw_g2": rw_g2[l].astype(BF16),
        "rw_k_k": _pairs(rw_k_k[l]), "rw_k_a": _pairs(rw_k_a[l]), "rw_r_k": _pairs(rw_r_k[l]),
        "rw_gn_g": _pairs(rw_gn_g[l]), "rw_gn_b": _pairs(rw_gn_b[l]),
        "w_o_attn": w_o_attn[l].astype(BF16), "w_o_rwkv": w_o_rwkv[l].astype(BF16), "w_out": w_out[l].astype(BF16),
        "ln1_g": row(ln1_g[l]), "ln1_b": row(ln1_b[l]),
        "ffn_w_gate": ffn_w_gate[l].astype(BF16), "ffn_w_up": ffn_w_up[l].astype(BF16),
        "ffn_w_down": ffn_w_down[l].astype(BF16), "ln2_g": row(ln2_g[l]), "ln2_b": row(ln2_b[l]),
    }

    tm = 512
    assert S % tm == 0
    x_meta = _pad_rows(meta_tokens.astype(F32), KEY_TILE, axis=0)
    m_q, m_k, m_v, m_kbf, m_vt, m_qi, m_kiwi, m_kibf, m_rw, m_wall = _proj(
        x_meta, _rope_table(jnp.arange(KEY_TILE)), 1, KEY_TILE, w)
    xf = x_prompt.reshape(B * S, D_MODEL)
    f_q, f_k, f_v, f_kbf, f_vt, f_qi, f_kiwi, f_kibf, f_rw, f_wall = _proj(
        xf, _rope_table(N_META + jnp.arange(S)), S // tm, tm, w)
    xs_pad = _pad_rows(x_sample, LANES, axis=1).reshape(Bs * LANES, D_MODEL)
    s_q, s_k, s_v, s_kbf, s_vt, s_qi, s_kiwi, s_kibf, s_rw, s_wall = _proj(
        xs_pad, _rope_table(N_META + P + jnp.arange(LANES)), 1, LANES, w)

    def key_rows(meta_tile, frames, total):
        b = frames.shape[0]
        mt = jnp.broadcast_to(meta_tile[None], (b,) + meta_tile.shape)
        return _pad_rows(jnp.concatenate([mt, frames], axis=1), total, axis=1)

    sx_p = -(-(KEY_TILE + S) // SCORE_ROWS) * SCORE_ROWS
    o_attn_p = _dsa(
        f_q, f_qi, f_wall,
        key_rows(m_kbf, f_kbf.reshape(B, S, HEAD_DIM), sx_p),
        key_rows(m_kibf, f_kibf.reshape(B, S, HEAD_DIM_I), sx_p),
        key_rows(m_vt, f_vt.reshape(B, S // KEY_TILE, HEAD_DIM, KEY_TILE), sx_p // KEY_TILE),
        n_batch=B, n_qblk=S // LANES, causal=True, n_sel=min(TOPK_MAX, S // 4), n_keys=S)

    n_keys_s = P + Ts
    sx_s = -(-(KEY_TILE + P + LANES) // SCORE_ROWS) * SCORE_ROWS
    meta_rows = lambda c: _pad_rows(c[l][:, :N_META], KEY_TILE, axis=1)
    past_vt = jnp.swapaxes(cache_v[l][:, N_META:].reshape(Bs, P // KEY_TILE, KEY_TILE, HEAD_DIM), -1, -2)
    meta_vt = jnp.swapaxes(meta_rows(cache_v), -1, -2)[:, None]
    kbf_s = _pad_rows(jnp.concatenate([meta_rows(cache_k).astype(BF16), cache_k[l][:, N_META:].astype(BF16),
                                       s_kbf.reshape(Bs, LANES, HEAD_DIM)], axis=1), sx_s)
    kibf_s = _pad_rows(jnp.concatenate([meta_rows(cache_idx_k).astype(BF16), cache_idx_k[l][:, N_META:].astype(BF16),
                                        s_kibf.reshape(Bs, LANES, HEAD_DIM_I)], axis=1), sx_s)
    vt_s = _pad_rows(jnp.concatenate([meta_vt.astype(BF16), past_vt.astype(BF16),
                                      s_vt.reshape(Bs, 1, HEAD_DIM, KEY_TILE)], axis=1), sx_s // KEY_TILE)
    o_attn_s = _dsa(s_q, s_qi, s_wall, kbf_s, kibf_s, vt_s,
                    n_batch=Bs, n_qblk=1, causal=False, n_sel=min(TOPK_MAX, n_keys_s // 4), n_keys=n_keys_s)
    o_attn_s = o_attn_s.reshape(Bs, LANES, D_MODEL)[:, :Ts].reshape(Bs * Ts, D_MODEL)

    zero_shift = jnp.zeros((1, 1, SHIFT_COLS), F32)
    zero_state = jnp.zeros((1, RW_PAIRS, LANES, LANES), F32)
    _, h_meta = _rwkv(m_rw[None], zero_shift, zero_state, w, n_chunks=1, chunk=N_META)
    shift_meta = m_rw[N_META - 1:N_META].reshape(1, 1, SHIFT_COLS)
    o_rw_p, h_p = _rwkv(f_rw.reshape(B, S, SHIFT_COLS), jnp.broadcast_to(shift_meta, (B, 1, SHIFT_COLS)),
                        jnp.broadcast_to(h_meta, (B, RW_PAIRS, LANES, LANES)), w, n_chunks=S // CHUNK, chunk=CHUNK)
    o_rw_s, h_s = _rwkv(s_rw.reshape(Bs, LANES, SHIFT_COLS), state_shift[l].reshape(Bs, 1, SHIFT_COLS),
                        _state_to_pairs(state_wkv[l]), w, n_chunks=1, chunk=CHUNK)

    y_p = _ffn(_merge(xf, o_attn_p, o_rw_p.reshape(B * S, D_MODEL), w, tm), w, tm)
    xs_flat = x_sample.reshape(Bs * Ts, D_MODEL)
    tm_s = min(tm, Bs * Ts)
    y_s = _ffn(_merge(xs_flat, o_attn_s, o_rw_s.reshape(Bs * Ts, D_MODEL), w, tm_s), w, tm_s)

    def with_meta(meta_rows_, frames, width):
        mt = jnp.broadcast_to(meta_rows_[None, :N_META, :width], (B, N_META, width))
        return jnp.concatenate([mt, frames.reshape(B, S, -1)[:, :, :width]], axis=1)[None]

    take = lambda a, width: a.reshape(Bs, LANES, -1)[:, :Ts, :width][None]
    return (
        y_p.reshape(B, S, D_MODEL),
        y_s.reshape(Bs, Ts, D_MODEL),
        with_meta(m_k, f_k, HEAD_DIM),
        with_meta(m_v, f_v, HEAD_DIM),
        with_meta(m_kiwi, f_kiwi, HEAD_DIM_I),
        _pairs_to_state(h_p)[None],
        f_rw.reshape(B, S, SHIFT_COLS)[:, -1][None],
        take(s_k, HEAD_DIM),
        take(s_v, HEAD_DIM),
        take(s_kiwi, HEAD_DIM_I),
        _pairs_to_state(h_s)[None],
        s_rw.reshape(Bs, LANES, SHIFT_COLS)[:, Ts - 1][None],
    )
```
